```python
import math
import jax, jax.numpy as jnp
from jax import lax
import numpy as np

D_MODEL = 2048
BATCH = 2
SEQ = 4096
DEPTH = 1

CHUNK = 64
MEM_LEN = 256
EPS = 1e-6
GDN_HEAD_DIM = 128
GDN_WIDTH = D_MODEL // 2
GDN_HEADS = GDN_WIDTH // GDN_HEAD_DIM
CONV_WIDTH = 4
S5_WIDTH = D_MODEL // 2
S5_GROUP = 16
S5_GROUPS = S5_WIDTH // S5_GROUP
S5_STATE = 64
XA_HEADS = 4
XA_WIDTH = D_MODEL // 2
XA_HEAD_DIM = XA_WIDTH // XA_HEADS
N_BRANCHES = 3
IN_SPLITS = (3 * GDN_WIDTH, GDN_WIDTH, GDN_HEADS, GDN_HEADS, S5_WIDTH, S5_WIDTH, XA_WIDTH, XA_WIDTH, N_BRANCHES * D_MODEL)

kernel_name = 'hybrid_gdn_s5_memxattn_block'


def rmsnorm(x, g):
    xf = x.astype(jnp.float32)
    y = xf * lax.rsqrt(jnp.mean(xf * xf, axis=-1, keepdims=True) + EPS)
    return (y * g.astype(jnp.float32)).astype(x.dtype)


def l2norm(x):
    return x * lax.rsqrt(jnp.sum(x * x, axis=-1, keepdims=True) + EPS)


def split_columns(t, sizes):
    out = []
    start = 0
    for s in sizes:
        out.append(t[..., start:start + s])
        start += s
    return out


def causal_depthwise_conv(x, w):
    c = x.shape[-1]
    return lax.conv_general_dilated(x, w[:, None, :], window_strides=(1,), padding=[(w.shape[0] - 1, 0)],
                                    dimension_numbers=('NWC', 'WIO', 'NWC'), feature_group_count=c)


def chunked_gated_delta_rule(q, k, v, g, beta):
    bsz, s, h, dk = q.shape
    dv = v.shape[-1]
    n = s // CHUNK
    def to_chunks(t):
        t = t.reshape((bsz, n, CHUNK, h) + t.shape[3:])
        return jnp.swapaxes(t, 2, 3)
    q, k, v, g, beta = (to_chunks(t) for t in (q, k, v, g, beta))
    gc = jnp.cumsum(g, axis=-1)
    incl = jnp.tril(jnp.ones((CHUNK, CHUNK), dtype=bool))
    strict = jnp.tril(jnp.ones((CHUNK, CHUNK), dtype=bool), -1)
    decay = jnp.exp(jnp.where(incl, gc[..., :, None] - gc[..., None, :], -jnp.inf))
    k_beta = k * beta[..., None]
    lower = jnp.where(strict, jnp.einsum('bnhid,bnhjd->bnhij', k_beta, k) * decay, 0.0)
    eye = jnp.eye(CHUNK, dtype=q.dtype)
    rhs = jnp.concatenate([v * beta[..., None], k_beta * jnp.exp(gc)[..., None]], axis=-1)
    sol = lax.linalg.triangular_solve(eye + lower, rhs, left_side=True, lower=True, unit_diagonal=True)
    u_val, w_dec = sol[..., :dv], sol[..., dv:]
    qk = jnp.einsum('bnhid,bnhjd->bnhij', q, k) * decay
    q_dec = q * jnp.exp(gc)[..., None]
    k_dec = k * jnp.exp(gc[..., -1:] - gc)[..., None]
    g_last = jnp.exp(gc[..., -1])

    def step(state, inp):
        w_n, u_n, qd_n, qk_n, kd_n, gl_n = inp
        v_new = u_n - jnp.einsum('bhcd,bhde->bhce', w_n, state)
        o = jnp.einsum('bhcd,bhde->bhce', qd_n, state) + jnp.einsum('bhij,bhje->bhie', qk_n, v_new)
        state = state * gl_n[..., None, None] + jnp.einsum('bhcd,bhce->bhde', kd_n, v_new)
        return state, o

    xs = tuple(jnp.moveaxis(t, 1, 0) for t in (w_dec, u_val, q_dec, qk, k_dec, g_last))
    state0 = jnp.zeros((bsz, h, dk, dv), dtype=q.dtype)
    _, o = lax.scan(step, state0, xs)
    o = jnp.transpose(o, (1, 0, 3, 2, 4))
    return o.reshape(bsz, s, h, dv)


def gated_deltanet(qkv_raw, beta_raw, a_raw, conv_w, a_log, dt_bias, out_norm_g):
    bsz, s, _ = qkv_raw.shape
    qkv = jax.nn.silu(causal_depthwise_conv(qkv_raw, conv_w)).astype(jnp.float32)
    q, k, v = jnp.split(qkv, 3, axis=-1)
    q = q.reshape(bsz, s, GDN_HEADS, GDN_HEAD_DIM)
    k = k.reshape(bsz, s, GDN_HEADS, GDN_HEAD_DIM)
    v = v.reshape(bsz, s, GDN_HEADS, GDN_HEAD_DIM)
    q = l2norm(q) * (GDN_HEAD_DIM ** -0.5)
    k = l2norm(k)
    beta = jax.nn.sigmoid(beta_raw.astype(jnp.float32))
    g = -jnp.exp(a_log.astype(jnp.float32)) * jax.nn.softplus(a_raw.astype(jnp.float32) + dt_bias.astype(jnp.float32))
    o = chunked_gated_delta_rule(q, k, v, g, beta)
    o = rmsnorm(o, out_norm_g)
    return o.reshape(bsz, s, GDN_WIDTH).astype(qkv_raw.dtype)


def complex_linear_combine(e1, e2):
    a1r, a1i, b1r, b1i = e1
    a2r, a2i, b2r, b2i = e2
    return (a2r * a1r - a2i * a1i,
            a2r * a1i + a2i * a1r,
            a2r * b1r - a2i * b1i + b2r,
            a2r * b1i + a2i * b1r + b2i)


def s5_layer(xb, lam_re, lam_im, log_dt, b_re, b_im, c_re, c_im, d, w_glu):
    f32 = jnp.float32
    bsz, s, _ = xb.shape
    xf = xb.astype(f32)
    xg = xf.reshape(bsz, s, S5_GROUPS, S5_GROUP)
    lr, li = lam_re.astype(f32), lam_im.astype(f32)
    dt = jnp.exp(log_dt.astype(f32))[:, None]
    mag = jnp.exp(lr * dt)
    ab_re, ab_im = mag * jnp.cos(li * dt), mag * jnp.sin(li * dt)
    den = lr * lr + li * li
    nr, ni = ab_re - 1.0, ab_im
    coef_re = (nr * lr + ni * li) / den
    coef_im = (ni * lr - nr * li) / den
    br, bi = b_re.astype(f32), b_im.astype(f32)
    bb_re = coef_re[..., None] * br - coef_im[..., None] * bi
    bb_im = coef_re[..., None] * bi + coef_im[..., None] * br
    bu_re = jnp.einsum('bsgi,gpi->bsgp', xg, bb_re)
    bu_im = jnp.einsum('bsgi,gpi->bsgp', xg, bb_im)
    a_re = jnp.broadcast_to(ab_re, bu_re.shape)
    a_im = jnp.broadcast_to(ab_im, bu_im.shape)
    _, _, h_re, h_im = lax.associative_scan(complex_linear_combine, (a_re, a_im, bu_re, bu_im), axis=1)
    y = jnp.einsum('bsgp,gip->bsgi', h_re, c_re.astype(f32)) - jnp.einsum('bsgp,gip->bsgi', h_im, c_im.astype(f32))
    y = y.reshape(bsz, s, S5_WIDTH) + d.astype(f32) * xf
    y = jax.nn.gelu(y)
    val, gate = jnp.split(y @ w_glu.astype(f32), 2, axis=-1)
    return (val * jax.nn.sigmoid(gate)).astype(xb.dtype)


def memory_cross_attention(q_raw, mem_n, w_kv):
    bsz, s, _ = q_raw.shape
    k, v = jnp.split(mem_n @ w_kv, 2, axis=-1)
    q = q_raw.reshape(bsz, s, XA_HEADS, XA_HEAD_DIM)
    k = k.reshape(bsz, MEM_LEN, XA_HEADS, XA_HEAD_DIM)
    v = v.reshape(bsz, MEM_LEN, XA_HEADS, XA_HEAD_DIM)
    scores = jnp.einsum('bshd,bmhd->bhsm', q, k).astype(jnp.float32) * (XA_HEAD_DIM ** -0.5)
    p = jax.nn.softmax(scores, axis=-1).astype(v.dtype)
    o = jnp.einsum('bhsm,bmhd->bshd', p, v)
    return o.reshape(bsz, s, XA_WIDTH)


def setup_inputs(seed: int = 0) -> dict:
    key = jax.random.key(seed)
    ks = jax.random.split(key, 28)
    f32 = jnp.float32
    L, G, P = DEPTH, S5_GROUPS, S5_STATE
    def nrm(k, shape, scale):
        return jax.random.normal(k, shape, f32) * scale
    in_width = sum(IN_SPLITS)
    dt = jnp.exp(jax.random.uniform(ks[6], (L, GDN_HEADS), f32, math.log(1e-3), math.log(1e-1)))
    n_idx = jnp.arange(P, dtype=f32)
    return {
        'x': nrm(ks[0], (BATCH, SEQ, D_MODEL), 1.0),
        'mem': nrm(ks[1], (BATCH, MEM_LEN, D_MODEL), 1.0),
        'norm_g': 1.0 + nrm(ks[2], (L, D_MODEL), 0.02),
        'w_in': nrm(ks[3], (L, D_MODEL, in_width), D_MODEL ** -0.5),
        'conv_w': nrm(ks[4], (L, CONV_WIDTH, 3 * GDN_WIDTH), CONV_WIDTH ** -0.5),
        'gdn_a_log': jnp.log(jax.random.uniform(ks[5], (L, GDN_HEADS), f32, 1.0, 16.0)),
        'gdn_dt_bias': dt + jnp.log(-jnp.expm1(-dt)),
        'gdn_norm_g': 1.0 + nrm(ks[7], (L, GDN_HEAD_DIM), 0.02),
        's5_lambda_re': -0.5 + nrm(ks[8], (L, G, P), 0.01),
        's5_lambda_im': math.pi * n_idx + nrm(ks[9], (L, G, P), 0.01),
        's5_log_dt': jax.random.uniform(ks[10], (L, G), f32, math.log(1e-3), math.log(1e-1)),
        's5_b_re': nrm(ks[11], (L, G, P, S5_GROUP), (2 * S5_GROUP) ** -0.5),
        's5_b_im': nrm(ks[12], (L, G, P, S5_GROUP), (2 * S5_GROUP) ** -0.5),
        's5_c_re': nrm(ks[13], (L, G, S5_GROUP, P), (2 * P) ** -0.5),
        's5_c_im': nrm(ks[14], (L, G, S5_GROUP, P), (2 * P) ** -0.5),
        's5_d': nrm(ks[15], (L, S5_WIDTH), 1.0),
        's5_w_glu': nrm(ks[16], (L, S5_WIDTH, 2 * S5_WIDTH), S5_WIDTH ** -0.5),
        'mem_norm_g': 1.0 + nrm(ks[17], (L, D_MODEL), 0.02),
        'w_kv_mem': nrm(ks[18], (L, D_MODEL, 2 * XA_WIDTH), D_MODEL ** -0.5),
        'w_br_a': nrm(ks[19], (L, GDN_WIDTH, D_MODEL), GDN_WIDTH ** -0.5),
        'w_br_b': nrm(ks[20], (L, S5_WIDTH, D_MODEL), S5_WIDTH ** -0.5),
        'w_br_c': nrm(ks[21], (L, XA_WIDTH, D_MODEL), XA_WIDTH ** -0.5),
        'w_out': nrm(ks[22], (L, D_MODEL, D_MODEL), D_MODEL ** -0.5),
        'final_g': 1.0 + nrm(ks[23], (D_MODEL,), 0.02),
    }


def reference(x, mem, norm_g, w_in, conv_w, gdn_a_log, gdn_dt_bias, gdn_norm_g,
              s5_lambda_re, s5_lambda_im, s5_log_dt, s5_b_re, s5_b_im, s5_c_re, s5_c_im, s5_d, s5_w_glu,
              mem_norm_g, w_kv_mem, w_br_a, w_br_b, w_br_c, w_out, final_g):
    bsz, s, _ = x.shape
    h = x
    for l in range(DEPTH):
        u = rmsnorm(h, norm_g[l])
        proj = u @ w_in[l]
        qkv_a, z_a, beta_raw, a_raw, x_b, z_b, q_c, z_c, gate_raw = split_columns(proj, IN_SPLITS)
        o_a = gated_deltanet(qkv_a, beta_raw, a_raw, conv_w[l], gdn_a_log[l], gdn_dt_bias[l], gdn_norm_g[l])
        p_a = (o_a * jax.nn.silu(z_a)) @ w_br_a[l]
        o_b = s5_layer(x_b, s5_lambda_re[l], s5_lambda_im[l], s5_log_dt[l], s5_b_re[l], s5_b_im[l],
                       s5_c_re[l], s5_c_im[l], s5_d[l], s5_w_glu[l])
        p_b = (o_b * jax.nn.silu(z_b)) @ w_br_b[l]
        o_c = memory_cross_attention(q_c, rmsnorm(mem, mem_norm_g[l]), w_kv_mem[l])
        p_c = (o_c * jax.nn.silu(z_c)) @ w_br_c[l]
        gates = jax.nn.sigmoid(gate_raw.astype(jnp.float32)).astype(x.dtype).reshape(bsz, s, N_BRANCHES, D_MODEL)
        merged = gates[:, :, 0] * p_a + gates[:, :, 1] * p_b + gates[:, :, 2] * p_c
        h = h + merged @ w_out[l]
    return rmsnorm(h, final_g)
```

```python
import functools
import math

import jax
import jax.numpy as jnp
from jax import lax
from jax.experimental import pallas as pl
from jax.experimental.pallas import tpu as pltpu

F32 = jnp.float32
BF16 = jnp.bfloat16
EPS = 1e-6

D_MODEL = 2048
CHUNK = 64
N_HEADS = 8
HEAD_DIM = 128
GDN_WIDTH = N_HEADS * HEAD_DIM
CONV_WIDTH = 4
S5_WIDTH = 1024
S5_GROUP = 16
S5_GROUPS = 64
S5_STATE = 64
S5_SLABS = 4
XA_HEADS = 4
XA_HEAD_DIM = 256
XA_WIDTH = 1024
MEM_LEN = 256
VMEM_LIMIT = 56 * 1024 * 1024

COL_QKV = 0
COL_ZA = 3
COL_XB = 4
COL_ZB = 5
COL_QC = 6
COL_ZC = 7
COL_GATES = 8
N_MAIN = 14 * 1024


def _mm(a, b):
    return jnp.dot(a.astype(BF16), b.astype(BF16), preferred_element_type=F32)


def _mm_nt(a, b):
    return lax.dot_general(a.astype(BF16), b.astype(BF16), (((1,), (1,)), ((), ())),
                           preferred_element_type=F32)


def _mm_tn(a, b):
    return lax.dot_general(a.astype(BF16), b.astype(BF16), (((0,), (0,)), ((), ())),
                           preferred_element_type=F32)


def _split3(x):
    x0 = x.astype(BF16)
    r = x - x0.astype(F32)
    x1 = r.astype(BF16)
    r = r - x1.astype(F32)
    return x0, x1, r.astype(BF16)


def _sel_rhs(x, sel):
    x0, x1, x2 = _split3(x)
    d = lambda t: jnp.dot(t, sel, preferred_element_type=F32)
    return d(x0) + d(x1) + d(x2)


def _sel_lhs(sel, x):
    x0, x1, x2 = _split3(x)
    d = lambda t: jnp.dot(sel, t, preferred_element_type=F32)
    return d(x0) + d(x1) + d(x2)


def _sigmoid(x):
    return 1.0 / (1.0 + jnp.exp(-x))


def _silu(x):
    return x * _sigmoid(x)


def _softplus(x):
    return jnp.maximum(x, 0.0) + jnp.log1p(jnp.exp(-jnp.abs(x)))


def _gelu_tanh(x):
    return 0.5 * x * (1.0 + jnp.tanh(math.sqrt(2.0 / math.pi) * (x + 0.044715 * (x * x * x))))


def _inproj_kernel(x_ref, g_ref, w_ref, ws_ref, wst_ref, o_ref, os_ref, ost_ref, u_scr):
    @pl.when(pl.program_id(1) == 0)
    def _():
        x = x_ref[...]
        ms = jnp.mean(x * x, axis=-1, keepdims=True)
        u = ((x * lax.rsqrt(ms + EPS)) * g_ref[...]).astype(BF16)
        u_scr[...] = u
        os_ref[...] = jnp.dot(u, ws_ref[...], preferred_element_type=F32)
        ost_ref[...] = lax.dot_general(wst_ref[...], u, (((1,), (1,)), ((), ())),
                                       preferred_element_type=F32)

    o_ref[...] = jnp.dot(u_scr[...], w_ref[...], preferred_element_type=F32)


def _in_projection(x2d, norm_g, w_main, w_small, w_small_t, tm=1024, tn=1024):
    t, d = x2d.shape
    n = w_main.shape[1]
    return pl.pallas_call(
        _inproj_kernel,
        grid=(t // tm, n // tn),
        in_specs=[
            pl.BlockSpec((tm, d), lambda i, j: (i, 0)),
            pl.BlockSpec((1, d), lambda i, j: (0, 0)),
            pl.BlockSpec((d, tn), lambda i, j: (0, j)),
            pl.BlockSpec((d, 128), lambda i, j: (0, 0)),
            pl.BlockSpec((16, d), lambda i, j: (0, 0)),
        ],
        out_specs=[
            pl.BlockSpec((tm, tn), lambda i, j: (i, j)),
            pl.BlockSpec((tm, 128), lambda i, j: (i, 0)),
            pl.BlockSpec((16, tm), lambda i, j: (0, i)),
        ],
        out_shape=[
            jax.ShapeDtypeStruct((t, n), F32),
            jax.ShapeDtypeStruct((t, 128), F32),
            jax.ShapeDtypeStruct((16, t), F32),
        ],
        scratch_shapes=[pltpu.VMEM((tm, d), BF16)],
        compiler_params=pltpu.CompilerParams(
            dimension_semantics=("parallel", "arbitrary"), vmem_limit_bytes=VMEM_LIMIT),
        name="in_projection",
    )(x2d, norm_g, w_main, w_small, w_small_t)


def _unit_lower_inverse(low, eye, blk16, blk32):
    n1 = jnp.where(blk16, -low, 0.0)
    n2 = _mm(n1, n1)
    n4 = _mm(n2, n2)
    n8 = _mm(n4, n4)
    p = eye + n1 + n2 + _mm(n1, n2)
    p = p + _mm(p, n4)
    p = p + _mm(p, n8)
    e1 = jnp.where(jnp.logical_and(blk32, jnp.logical_not(blk16)), low, 0.0)
    p = p - _mm(_mm(p, e1), p)
    e2 = jnp.where(blk32, 0.0, low)
    return p - _mm(_mm(p, e2), p)


def _gdn_kernel(qkv_ref, za_ref, sm_ref, smt_ref, convw_ref, expand_ref, alog_row_ref, dtb_row_ref,
                alog_col_ref, dtb_col_ref, ng_ref, o_ref, ext_scr, state_scr):
    c = CHUNK

    @pl.when(pl.program_id(1) == 0)
    def _():
        ext_scr[0:8, :] = jnp.zeros((8, 3 * GDN_WIDTH), F32)
        state_scr[...] = jnp.zeros_like(state_scr)

    raw = qkv_ref[...]
    ext_scr[8:8 + c, :] = raw
    conv = convw_ref[3:4, :] * raw
    for j in range(CONV_WIDTH - 1):
        conv = conv + convw_ref[j:j + 1, :] * ext_scr[5 + j:5 + j + c, :]
    ext_scr[0:8, :] = raw[c - 8:c, :]
    qkv = _silu(conv)

    row = lax.broadcasted_iota(jnp.int32, (c, c), 0)
    col = lax.broadcasted_iota(jnp.int32, (c, c), 1)
    incl = row >= col
    strict = row > col
    blk16 = (row >> 4) == (col >> 4)
    blk32 = (row >> 5) == (col >> 5)
    eye = jnp.where(row == col, 1.0, 0.0).astype(F32)
    tri_lo = jnp.where(incl, 1.0, 0.0).astype(BF16)
    tri_up = jnp.where(row <= col, 1.0, 0.0).astype(BF16)

    rawb = _sel_rhs(sm_ref[...], expand_ref[...])
    beta_b = _sigmoid(rawb[:, :GDN_WIDTH])
    g_b = -jnp.exp(alog_row_ref[...]) * _softplus(rawb[:, GDN_WIDTH:] + dtb_row_ref[...])
    gc_b = _sel_lhs(tri_lo, g_b)
    g_r = -jnp.exp(alog_col_ref[...]) * _softplus(smt_ref[0] + dtb_col_ref[...])
    gc_r = _sel_rhs(g_r, tri_up)[N_HEADS:, :]

    for h in range(N_HEADS):
        lo = h * HEAD_DIM
        q = qkv[:, lo:lo + HEAD_DIM]
        k = qkv[:, GDN_WIDTH + lo:GDN_WIDTH + lo + HEAD_DIM]
        v = qkv[:, 2 * GDN_WIDTH + lo:2 * GDN_WIDTH + lo + HEAD_DIM]
        q = (q * lax.rsqrt(jnp.sum(q * q, axis=-1, keepdims=True) + EPS)) * (HEAD_DIM ** -0.5)
        k = k * lax.rsqrt(jnp.sum(k * k, axis=-1, keepdims=True) + EPS)
        beta = beta_b[:, lo:lo + HEAD_DIM]
        gcb = gc_b[:, lo:lo + HEAD_DIM]
        egc = jnp.exp(gcb)
        gcl = gcb[c - 1:c, :]
        kb = k * beta
        aq = _mm_nt(jnp.concatenate([kb, q], axis=0), k)
        decay = jnp.exp(jnp.where(incl, gcb[:, :c] - gc_r[h:h + 1, :], -jnp.inf))
        low = jnp.where(strict, aq[:c] * decay, 0.0)
        tinv = _unit_lower_inverse(low, eye, blk16, blk32)
        sol = _mm(tinv, jnp.concatenate([v * beta, kb * egc], axis=1))
        u_val = sol[:, :HEAD_DIM]
        w_dec = sol[:, HEAD_DIM:]
        state = state_scr[h]
        ws = _mm(jnp.concatenate([w_dec, q * egc], axis=0), state)
        v_new = u_val - ws[:c]
        o = ws[c:] + _mm(aq[c:] * decay, v_new)
        k_dec = k * jnp.exp(gcl - gcb)
        state_scr[h] = state * jnp.exp(gcl) + _mm_tn(k_dec, v_new)
        o = (o * lax.rsqrt(jnp.mean(o * o, axis=-1, keepdims=True) + EPS)) * ng_ref[...]
        o_ref[:, lo:lo + HEAD_DIM] = (o * _silu(za_ref[:, lo:lo + HEAD_DIM])).astype(o_ref.dtype)


def _gdn_branch(proj, small, small_t3, conv_w, expand, alog_row, dtb_row, alog_col, dtb_col, norm_g,
                bsz, seq):
    nc = seq // CHUNK
    t = bsz * seq
    full = lambda shape: pl.BlockSpec(shape, lambda b, c: (0,) * len(shape))
    return pl.pallas_call(
        _gdn_kernel,
        grid=(bsz, nc),
        in_specs=[
            pl.BlockSpec((CHUNK, 3 * GDN_WIDTH), lambda b, c: (b * nc + c, COL_QKV)),
            pl.BlockSpec((CHUNK, GDN_WIDTH), lambda b, c: (b * nc + c, COL_ZA)),
            pl.BlockSpec((CHUNK, 128), lambda b, c: (b * nc + c, 0)),
            pl.BlockSpec((1, 16, CHUNK), lambda b, c: (b * nc + c, 0, 0)),
            full((CONV_WIDTH, 3 * GDN_WIDTH)),
            full((128, 2 * GDN_WIDTH)),
            full((1, GDN_WIDTH)),
            full((1, GDN_WIDTH)),
            full((2 * N_HEADS, 1)),
            full((2 * N_HEADS, 1)),
            full((1, HEAD_DIM)),
        ],
        out_specs=pl.BlockSpec((CHUNK, GDN_WIDTH), lambda b, c: (b * nc + c, 0)),
        out_shape=jax.ShapeDtypeStruct((t, GDN_WIDTH), BF16),
        scratch_shapes=[
            pltpu.VMEM((CHUNK + 8, 3 * GDN_WIDTH), F32),
            pltpu.VMEM((N_HEADS, HEAD_DIM, HEAD_DIM), F32),
        ],
        compiler_params=pltpu.CompilerParams(
            dimension_semantics=("parallel", "arbitrary"), vmem_limit_bytes=VMEM_LIMIT),
        name="gated_deltanet",
    )(proj, proj, small, small_t3, conv_w, expand, alog_row, dtb_row, alog_col, dtb_col, norm_g)


def _s5_kernel(xb_ref, zb_ref, bdb_ref, bdc_ref, tab_ref, d_ref, wglu_ref, o_ref,
               hre_scr, him_scr, carry_scr, y_scr):
    tt = xb_ref.shape[0]
    ns = S5_GROUPS * S5_STATE // S5_SLABS

    @pl.when(pl.program_id(1) == 0)
    def _():
        carry_scr[...] = jnp.zeros_like(carry_scr)

    x = xb_ref[...]
    for s in range(S5_SLABS):
        bu = jnp.dot(x[:, s * 256:(s + 1) * 256].astype(BF16), bdb_ref[s], preferred_element_type=F32)
        hre_scr[...] = bu[:, :ns]
        him_scr[...] = bu[:, ns:]

        def body(j, carry, s=s):
            cr, ci = carry
            r0 = pl.multiple_of(j * 8, 8)
            br = hre_scr[pl.ds(r0, 8), :]
            bi = him_scr[pl.ds(r0, 8), :]
            for lvl, dist in enumerate((1, 2, 4)):
                tr = tab_ref[s, lvl, 0]
                ti = tab_ref[s, lvl, 1]
                sr = pltpu.roll(br, dist, 0)
                si = pltpu.roll(bi, dist, 0)
                br, bi = br + (tr * sr - ti * si), bi + (tr * si + ti * sr)
            pr = tab_ref[s, 3, 0]
            pi = tab_ref[s, 3, 1]
            br, bi = br + (pr * cr - pi * ci), bi + (pr * ci + pi * cr)
            hre_scr[pl.ds(r0, 8), :] = br
            him_scr[pl.ds(r0, 8), :] = bi
            return (jnp.broadcast_to(br[7:8, :], (8, ns)), jnp.broadcast_to(bi[7:8, :], (8, ns)))

        cr, ci = lax.fori_loop(0, tt // 8, body, (carry_scr[s, 0], carry_scr[s, 1]))
        carry_scr[s, 0] = cr
        carry_scr[s, 1] = ci
        y_scr[:, s * 256:(s + 1) * 256] = (
            jnp.dot(hre_scr[...].astype(BF16), bdc_ref[s, :ns, :], preferred_element_type=F32)
            + jnp.dot(him_scr[...].astype(BF16), bdc_ref[s, ns:, :], preferred_element_type=F32))

    y = _gelu_tanh(y_scr[...] + d_ref[...] * x)
    glu = jnp.dot(y.astype(BF16), wglu_ref[...], preferred_element_type=F32)
    o_b = glu[:, :S5_WIDTH] * _sigmoid(glu[:, S5_WIDTH:])
    o_ref[...] = (o_b * _silu(zb_ref[...])).astype(o_ref.dtype)


def _s5_branch(proj, bdb, bdc, tabs, d_row, w_glu, bsz, seq, tt=256):
    nt = seq // tt
    ns = S5_GROUPS * S5_STATE // S5_SLABS
    full = lambda shape: pl.BlockSpec(shape, lambda b, i: (0,) * len(shape))
    return pl.pallas_call(
        _s5_kernel,
        grid=(bsz, nt),
        in_specs=[
            pl.BlockSpec((tt, S5_WIDTH), lambda b, i: (b * nt + i, COL_XB)),
            pl.BlockSpec((tt, S5_WIDTH), lambda b, i: (b * nt + i, COL_ZB)),
            full((S5_SLABS, 256, 2 * ns)),
            full((S5_SLABS, 2 * ns, 256)),
            full((S5_SLABS, 4, 2, 8, ns)),
            full((1, S5_WIDTH)),
            full((S5_WIDTH, 2 * S5_WIDTH)),
        ],
        out_specs=pl.BlockSpec((tt, S5_WIDTH), lambda b, i: (b * nt + i, 0)),
        out_shape=jax.ShapeDtypeStruct((bsz * seq, S5_WIDTH), BF16),
        scratch_shapes=[
            pltpu.VMEM((tt, ns), F32),
            pltpu.VMEM((tt, ns), F32),
            pltpu.VMEM((S5_SLABS, 2, 8, ns), F32),
            pltpu.VMEM((tt, S5_WIDTH), F32),
        ],
        compiler_params=pltpu.CompilerParams(
            dimension_semantics=("parallel", "arbitrary"), vmem_limit_bytes=VMEM_LIMIT),
        name="s5_layer",
    )(proj, proj, bdb, bdc, tabs, d_row, w_glu)


def _s5_tables(lam_re, lam_im, log_dt, b_re, b_im, c_re, c_im):
    g, p, gi = S5_GROUPS, S5_STATE, S5_GROUP
    gs = g // S5_SLABS
    lr, li = lam_re.astype(F32), lam_im.astype(F32)
    dt = jnp.exp(log_dt.astype(F32))[:, None]
    mag = jnp.exp(lr * dt)
    ab_re, ab_im = mag * jnp.cos(li * dt), mag * jnp.sin(li * dt)
    den = lr * lr + li * li
    nr, ni = ab_re - 1.0, ab_im
    coef_re = (nr * lr + ni * li) / den
    coef_im = (ni * lr - nr * li) / den
    br, bi = b_re.astype(F32), b_im.astype(F32)
    bb_re = coef_re[..., None] * br - coef_im[..., None] * bi
    bb_im = coef_re[..., None] * bi + coef_im[..., None] * br

    eye = jnp.eye(gs, dtype=F32)

    def blockdiag_in(bb):
        bb = bb.reshape(S5_SLABS, gs, p, gi)
        return jnp.einsum('sgpi,gh->sgihp', bb, eye).reshape(S5_SLABS, gs * gi, gs * p)

    def blockdiag_out(cc):
        cc = cc.reshape(S5_SLABS, gs, gi, p)
        return jnp.einsum('sgip,gh->sgphi', cc, eye).reshape(S5_SLABS, gs * p, gs * gi)

    bdb = jnp.concatenate([blockdiag_in(bb_re), blockdiag_in(bb_im)], axis=2).astype(BF16)
    bdc = jnp.concatenate([blockdiag_out(c_re.astype(F32)), -blockdiag_out(c_im.astype(F32))],
                          axis=1).astype(BF16)

    def cmul(a, b):
        return a[0] * b[0] - a[1] * b[1], a[0] * b[1] + a[1] * b[0]

    p1 = (ab_re.reshape(S5_SLABS, gs * p), ab_im.reshape(S5_SLABS, gs * p))
    p2 = cmul(p1, p1)
    p4 = cmul(p2, p2)
    rows = jnp.arange(8)[None, :, None]

    def masked(pw, dist):
        t = jnp.stack([pw[0], pw[1]], axis=1)[:, :, None, :]
        return jnp.where(rows[None] >= dist, jnp.broadcast_to(t, (S5_SLABS, 2, 8, gs * p)), 0.0)

    pows = [p1]
    for _ in range(7):
        pows.append(cmul(pows[-1], p1))
    ramp = jnp.stack([jnp.stack([pw[0] for pw in pows], axis=1),
                      jnp.stack([pw[1] for pw in pows], axis=1)], axis=1)
    tabs = jnp.stack([masked(p1, 1), masked(p2, 2), masked(p4, 4), ramp], axis=1)
    return bdb, bdc, tabs


def _kv_kernel(mem_ref, g_ref, w_ref, o_ref):
    m = mem_ref[0]
    ms = jnp.mean(m * m, axis=-1, keepdims=True)
    mn = ((m * lax.rsqrt(ms + EPS)) * g_ref[...]).astype(BF16)
    o_ref[0] = jnp.dot(mn, w_ref[...], preferred_element_type=F32).astype(o_ref.dtype)


def _memory_kv(mem, mem_norm_g, w_kv):
    bsz = mem.shape[0]
    return pl.pallas_call(
        _kv_kernel,
        grid=(bsz,),
        in_specs=[
            pl.BlockSpec((1, MEM_LEN, D_MODEL), lambda b: (b, 0, 0)),
            pl.BlockSpec((1, D_MODEL), lambda b: (0, 0)),
            pl.BlockSpec((D_MODEL, 2 * XA_WIDTH), lambda b: (0, 0)),
        ],
        out_specs=pl.BlockSpec((1, MEM_LEN, 2 * XA_WIDTH), lambda b: (b, 0, 0)),
        out_shape=jax.ShapeDtypeStruct((bsz, MEM_LEN, 2 * XA_WIDTH), BF16),
        compiler_params=pltpu.CompilerParams(
            dimension_semantics=("parallel",), vmem_limit_bytes=VMEM_LIMIT),
        name="memory_kv",
    )(mem, mem_norm_g, w_kv)


def _xattn_kernel(q_ref, zc_ref, kv_ref, o_ref):
    scale = XA_HEAD_DIM ** -0.5
    for h in range(XA_HEADS):
        lo = h * XA_HEAD_DIM
        q = q_ref[:, lo:lo + XA_HEAD_DIM].astype(BF16)
        k = kv_ref[0, :, lo:lo + XA_HEAD_DIM]
        v = kv_ref[0, :, XA_WIDTH + lo:XA_WIDTH + lo + XA_HEAD_DIM]
        s = lax.dot_general(q, k, (((1,), (1,)), ((), ())), preferred_element_type=F32) * scale
        e = jnp.exp(s - jnp.max(s, axis=-1, keepdims=True))
        p = e / jnp.sum(e, axis=-1, keepdims=True)
        o = jnp.dot(p.astype(BF16), v, preferred_element_type=F32)
        o_ref[:, lo:lo + XA_HEAD_DIM] = (o * _silu(zc_ref[:, lo:lo + XA_HEAD_DIM])).astype(o_ref.dtype)


def _xattn_branch(proj, kv, bsz, seq, tt=512):
    nt = seq // tt
    return pl.pallas_call(
        _xattn_kernel,
        grid=(bsz, nt),
        in_specs=[
            pl.BlockSpec((tt, XA_WIDTH), lambda b, i: (b * nt + i, COL_QC)),
            pl.BlockSpec((tt, XA_WIDTH), lambda b, i: (b * nt + i, COL_ZC)),
            pl.BlockSpec((1, MEM_LEN, 2 * XA_WIDTH), lambda b, i: (b, 0, 0)),
        ],
        out_specs=pl.BlockSpec((tt, XA_WIDTH), lambda b, i: (b * nt + i, 0)),
        out_shape=jax.ShapeDtypeStruct((bsz * seq, XA_WIDTH), BF16),
        compiler_params=pltpu.CompilerParams(
            dimension_semantics=("parallel", "parallel"), vmem_limit_bytes=VMEM_LIMIT),
        name="memory_xattn",
    )(proj, proj, kv)


def _merge_kernel(a_ref, b_ref, c_ref, g0_ref, g1_ref, g2_ref, x_ref, wa_ref, wb_ref, wc_ref,
                  wo_ref, fg_ref, o_ref):
    merged = _sigmoid(g0_ref[...]) * jnp.dot(a_ref[...], wa_ref[...], preferred_element_type=F32)
    merged = merged + _sigmoid(g1_ref[...]) * jnp.dot(b_ref[...], wb_ref[...], preferred_element_type=F32)
    merged = merged + _sigmoid(g2_ref[...]) * jnp.dot(c_ref[...], wc_ref[...], preferred_element_type=F32)
    h = x_ref[...] + jnp.dot(merged.astype(BF16), wo_ref[...], preferred_element_type=F32)
    ms = jnp.mean(h * h, axis=-1, keepdims=True)
    o_ref[...] = (h * lax.rsqrt(ms + EPS)) * fg_ref[...]


def _merge_output(a, b, c, proj, x2d, wa, wb, wc, wo, final_g, tm=256):
    t = x2d.shape[0]
    gb = COL_GATES // 2
    const = lambda shape: pl.BlockSpec(shape, lambda i: (0,) * len(shape), pipeline_mode=pl.Buffered(1))
    return pl.pallas_call(
        _merge_kernel,
        grid=(t // tm,),
        in_specs=[
            pl.BlockSpec((tm, GDN_WIDTH), lambda i: (i, 0)),
            pl.BlockSpec((tm, S5_WIDTH), lambda i: (i, 0)),
            pl.BlockSpec((tm, XA_WIDTH), lambda i: (i, 0)),
            pl.BlockSpec((tm, D_MODEL), lambda i: (i, gb)),
            pl.BlockSpec((tm, D_MODEL), lambda i: (i, gb + 1)),
            pl.BlockSpec((tm, D_MODEL), lambda i: (i, gb + 2)),
            pl.BlockSpec((tm, D_MODEL), lambda i: (i, 0)),
            const((GDN_WIDTH, D_MODEL)),
            const((S5_WIDTH, D_MODEL)),
            const((XA_WIDTH, D_MODEL)),
            const((D_MODEL, D_MODEL)),
            const((1, D_MODEL)),
        ],
        out_specs=pl.BlockSpec((tm, D_MODEL), lambda i: (i, 0)),
        out_shape=jax.ShapeDtypeStruct((t, D_MODEL), F32),
        compiler_params=pltpu.CompilerParams(
            dimension_semantics=("parallel",), vmem_limit_bytes=VMEM_LIMIT),
        name="merge_output",
    )(a, b, c, proj, proj, proj, x2d, wa, wb, wc, wo, final_g)


def kernel(x, mem, norm_g, w_in, conv_w, gdn_a_log, gdn_dt_bias, gdn_norm_g, s5_lambda_re, s5_lambda_im,
           s5_log_dt, s5_b_re, s5_b_im, s5_c_re, s5_c_im, s5_d, s5_w_glu, mem_norm_g, w_kv_mem, w_br_a,
           w_br_b, w_br_c, w_out, final_g):
    bsz, seq, d = x.shape
    t = bsz * seq
    h = x.reshape(t, d)
    assert norm_g.shape[0] == 1, "the merge kernel fuses the final RMSNorm: single layer only"
    for l in range(1):
        wl = w_in[l]
        n_a = 4 * GDN_WIDTH
        w_main = jnp.concatenate([wl[:, :n_a], wl[:, n_a + 2 * N_HEADS:]], axis=1).astype(BF16)
        w_small = wl[:, n_a:n_a + 2 * N_HEADS]
        w_small_t = w_small.T.astype(BF16)
        w_small = jnp.pad(w_small, ((0, 0), (0, 128 - 2 * N_HEADS))).astype(BF16)

        proj, small, small_t = _in_projection(h, norm_g[l][None, :], w_main, w_small, w_small_t)

        small_t3 = small_t.reshape(2 * N_HEADS, t // CHUNK, CHUNK).transpose(1, 0, 2)
        expand = (jnp.arange(2 * GDN_WIDTH)[None, :] // HEAD_DIM == jnp.arange(128)[:, None]).astype(BF16)
        alog = gdn_a_log[l].astype(F32)
        dtb = gdn_dt_bias[l].astype(F32)
        o_a = _gdn_branch(proj, small, small_t3, conv_w[l], expand,
                          jnp.repeat(alog, HEAD_DIM)[None, :], jnp.repeat(dtb, HEAD_DIM)[None, :],
                          jnp.pad(alog, (N_HEADS, 0))[:, None], jnp.pad(dtb, (N_HEADS, 0))[:, None],
                          gdn_norm_g[l][None, :], bsz, seq)

        bdb, bdc, tabs = _s5_tables(s5_lambda_re[l], s5_lambda_im[l], s5_log_dt[l], s5_b_re[l], s5_b_im[l],
                                    s5_c_re[l], s5_c_im[l])
        o_b = _s5_branch(proj, bdb, bdc, tabs, s5_d[l][None, :], s5_w_glu[l].astype(BF16), bsz, seq)

        kv = _memory_kv(mem, mem_norm_g[l][None, :], w_kv_mem[l].astype(BF16))
        o_c = _xattn_branch(proj, kv, bsz, seq)

        out = _merge_output(o_a, o_b, o_c, proj, h, w_br_a[l].astype(BF16), w_br_b[l].astype(BF16),
                            w_br_c[l].astype(BF16), w_out[l].astype(BF16), final_g[None, :])
    return out.reshape(bsz, seq, d)
```

```python
import functools
import math

import jax
import jax.numpy as jnp
from jax import lax
from jax.experimental import pallas as pl
from jax.experimental.pallas import tpu as pltpu

F32 = jnp.float32
BF16 = jnp.bfloat16
EPS = 1e-6

D_MODEL = 2048
CHUNK = 64
N_HEADS = 8
HEAD_DIM = 128
GDN_WIDTH = N_HEADS * HEAD_DIM
CONV_WIDTH = 4
S5_WIDTH = 1024
S5_GROUP = 16
S5_GROUPS = 64
S5_STATE = 64
S5_SLABS = 4
XA_HEADS = 4
XA_HEAD_DIM = 256
XA_WIDTH = 1024
MEM_LEN = 256
VMEM_LIMIT = 56 * 1024 * 1024

COL_QKV = 0
COL_ZA = 3
COL_XB = 4
COL_ZB = 5
COL_QC = 6
COL_ZC = 7
COL_GATES = 8
N_MAIN = 14 * 1024


def _mm(a, b):
    return jnp.dot(a.astype(BF16), b.astype(BF16), preferred_element_type=F32)


def _mm_nt(a, b):
    return lax.dot_general(a.astype(BF16), b.astype(BF16), (((1,), (1,)), ((), ())),
                           preferred_element_type=F32)


def _mm_tn(a, b):
    return lax.dot_general(a.astype(BF16), b.astype(BF16), (((0,), (0,)), ((), ())),
                           preferred_element_type=F32)


def _split3(x):
    x0 = x.astype(BF16)
    r = x - x0.astype(F32)
    x1 = r.astype(BF16)
    r = r - x1.astype(F32)
    return x0, x1, r.astype(BF16)


def _sel_rhs(x, sel):
    x0, x1, x2 = _split3(x)
    d = lambda t: jnp.dot(t, sel, preferred_element_type=F32)
    return d(x0) + d(x1) + d(x2)


def _sel_lhs(sel, x):
    x0, x1, x2 = _split3(x)
    d = lambda t: jnp.dot(sel, t, preferred_element_type=F32)
    return d(x0) + d(x1) + d(x2)


def _sigmoid(x):
    return 1.0 / (1.0 + jnp.exp(-x))


def _silu(x):
    return x * _sigmoid(x)


def _softplus(x):
    return jnp.maximum(x, 0.0) + jnp.log1p(jnp.exp(-jnp.abs(x)))


def _gelu_tanh(x):
    return 0.5 * x * (1.0 + jnp.tanh(math.sqrt(2.0 / math.pi) * (x + 0.044715 * (x * x * x))))


def _inproj_kernel(x_ref, g_ref, w_ref, ws_ref, wst_ref, o_ref, os_ref, ost_ref, u_scr):
    @pl.when(pl.program_id(1) == 0)
    def _():
        x = x_ref[...]
        ms = jnp.mean(x * x, axis=-1, keepdims=True)
        u = ((x * lax.rsqrt(ms + EPS)) * g_ref[...]).astype(BF16)
        u_scr[...] = u
        os_ref[...] = jnp.dot(u, ws_ref[...], preferred_element_type=F32)
        ost_ref[...] = lax.dot_general(wst_ref[...], u, (((1,), (1,)), ((), ())),
                                       preferred_element_type=F32)

    o_ref[...] = jnp.dot(u_scr[...], w_ref[...], preferred_element_type=F32)


def _in_projection(x2d, norm_g, w_main, w_small, w_small_t, tm=1024, tn=1024):
    t, d = x2d.shape
    n = w_main.shape[1]
    return pl.pallas_call(
        _inproj_kernel,
        grid=(t // tm, n // tn),
        in_specs=[
            pl.BlockSpec((tm, d), lambda i, j: (i, 0)),
            pl.BlockSpec((1, d), lambda i, j: (0, 0)),
            pl.BlockSpec((d, tn), lambda i, j: (0, j)),
            pl.BlockSpec((d, 128), lambda i, j: (0, 0)),
            pl.BlockSpec((16, d), lambda i, j: (0, 0)),
        ],
        out_specs=[
            pl.BlockSpec((tm, tn), lambda i, j: (i, j)),
            pl.BlockSpec((tm, 128), lambda i, j: (i, 0)),
            pl.BlockSpec((16, tm), lambda i, j: (0, i)),
        ],
        out_shape=[
            jax.ShapeDtypeStruct((t, n), F32),
            jax.ShapeDtypeStruct((t, 128), F32),
            jax.ShapeDtypeStruct((16, t), F32),
        ],
        scratch_shapes=[pltpu.VMEM((tm, d), BF16)],
        compiler_params=pltpu.CompilerParams(
            dimension_semantics=("parallel", "arbitrary"), vmem_limit_bytes=VMEM_LIMIT),
        name="in_projection",
    )(x2d, norm_g, w_main, w_small, w_small_t)


def _each(f, *lists):
    return [f(*args) for args in zip(*lists)]


def _unit_lower_inverse(lows, eye, blk16, blk32):
    n1 = _each(lambda l: jnp.where(blk16, -l, 0.0), lows)
    n2 = _each(_mm, n1, n1)
    n4 = _each(_mm, n2, n2)
    n3 = _each(_mm, n1, n2)
    n8 = _each(_mm, n4, n4)
    p = _each(lambda a, b, d: eye + a + b + d, n1, n2, n3)
    p = _each(lambda a, b: a + _mm(a, b), p, n4)
    p = _each(lambda a, b: a + _mm(a, b), p, n8)
    for mask in (jnp.logical_and(blk32, jnp.logical_not(blk16)), jnp.logical_not(blk32)):
        pe = _each(lambda a, l: _mm(a, jnp.where(mask, l, 0.0)), p, lows)
        p = _each(lambda a, b: a - _mm(b, a), p, pe)
    return p


def _gdn_kernel(qkv_ref, za_ref, sm_ref, smt_ref, convw_ref, expand_ref, alog_row_ref, dtb_row_ref,
                alog_col_ref, dtb_col_ref, ng_ref, o_ref, ext_scr, state_scr):
    c = CHUNK
    tt = qkv_ref.shape[0]
    nch = tt // c

    @pl.when(pl.program_id(1) == 0)
    def _():
        ext_scr[0:8, :] = jnp.zeros((8, 3 * GDN_WIDTH), F32)
        state_scr[...] = jnp.zeros_like(state_scr)

    raw = qkv_ref[...]
    ext_scr[8:8 + tt, :] = raw
    conv = convw_ref[3:4, :] * raw
    for j in range(CONV_WIDTH - 1):
        conv = conv + convw_ref[j:j + 1, :] * ext_scr[5 + j:5 + j + tt, :]
    ext_scr[0:8, :] = raw[tt - 8:tt, :]
    qkv = _silu(conv)

    row = lax.broadcasted_iota(jnp.int32, (c, c), 0)
    col = lax.broadcasted_iota(jnp.int32, (c, c), 1)
    incl = row >= col
    strict = row > col
    blk16 = (row >> 4) == (col >> 4)
    blk32 = (row >> 5) == (col >> 5)
    eye = jnp.where(row == col, 1.0, 0.0).astype(F32)
    tri_up = jnp.where(row <= col, 1.0, 0.0).astype(BF16)
    rt = lax.broadcasted_iota(jnp.int32, (tt, tt), 0)
    ct = lax.broadcasted_iota(jnp.int32, (tt, tt), 1)
    tri_tile = jnp.where(jnp.logical_and(rt >= ct, (rt >> 6) == (ct >> 6)), 1.0, 0.0).astype(BF16)

    rawb = _sel_rhs(sm_ref[...], expand_ref[...])
    beta_b = _sigmoid(rawb[:, :GDN_WIDTH])
    g_b = -jnp.exp(alog_row_ref[...]) * _softplus(rawb[:, GDN_WIDTH:] + dtb_row_ref[...])
    gc_b = _sel_lhs(tri_tile, g_b)
    egc_b = jnp.exp(gc_b)
    gc_r = []
    for ci in range(nch):
        g_r = -jnp.exp(alog_col_ref[...]) * _softplus(smt_ref[ci] + dtb_col_ref[...])
        gc_r.append(_sel_rhs(g_r, tri_up)[N_HEADS:, :])

    chains = [(ci, h) for ci in range(nch) for h in range(N_HEADS)]

    def part(arr, ci, h, base=0):
        return arr[ci * c:(ci + 1) * c, base + h * HEAD_DIM:base + (h + 1) * HEAD_DIM]

    def l2n(a):
        return a * lax.rsqrt(jnp.sum(a * a, axis=-1, keepdims=True) + EPS)

    q = [l2n(part(qkv, ci, h)) * (HEAD_DIM ** -0.5) for ci, h in chains]
    k = [l2n(part(qkv, ci, h, GDN_WIDTH)) for ci, h in chains]
    v = [part(qkv, ci, h, 2 * GDN_WIDTH) for ci, h in chains]
    beta = [part(beta_b, ci, h) for ci, h in chains]
    gcb = [part(gc_b, ci, h) for ci, h in chains]
    egc = [part(egc_b, ci, h) for ci, h in chains]
    kb = _each(lambda a, b: a * b, k, beta)
    aq = _each(lambda a, b, d: _mm_nt(jnp.concatenate([a, b], axis=0), d), kb, q, k)
    decay = [jnp.exp(jnp.where(incl, g[:, :c] - gc_r[ci][h:h + 1, :], -jnp.inf))
             for g, (ci, h) in zip(gcb, chains)]
    low = _each(lambda a, d: jnp.where(strict, a[:c] * d, 0.0), aq, decay)
    tinv = _unit_lower_inverse(low, eye, blk16, blk32)
    sol = _each(lambda t, a, b, d, e: _mm(t, jnp.concatenate([a * b, d * e], axis=1)),
                tinv, v, beta, kb, egc)
    wq = _each(lambda s, a, e: jnp.concatenate([s[:, HEAD_DIM:], a * e], axis=0), sol, q, egc)
    qkd = _each(lambda a, d: a[c:] * d, aq, decay)
    k_dec = _each(lambda a, g: a * jnp.exp(g[c - 1:c, :] - g), k, gcb)
    g_last = [jnp.exp(g[c - 1:c, :]) for g in gcb]

    states = [state_scr[h] for h in range(N_HEADS)]
    outs = []
    for ci in range(nch):
        idx = [ci * N_HEADS + h for h in range(N_HEADS)]
        ws = [_mm(wq[i], s) for i, s in zip(idx, states)]
        v_new = [sol[i][:, :HEAD_DIM] - w[:c] for i, w in zip(idx, ws)]
        o = [w[c:] + _mm(qkd[i], vn) for i, w, vn in zip(idx, ws, v_new)]
        states = [s * g_last[i] + _mm_tn(k_dec[i], vn) for i, s, vn in zip(idx, states, v_new)]
        o = [a * lax.rsqrt(jnp.mean(a * a, axis=-1, keepdims=True) + EPS) for a in o]
        outs.append(jnp.concatenate(o, axis=1))
    o_tile = jnp.concatenate(outs, axis=0) if nch > 1 else outs[0]
    o_ref[...] = ((o_tile * ng_ref[...]) * _silu(za_ref[...])).astype(o_ref.dtype)
    for h in range(N_HEADS):
        state_scr[h] = states[h]


def _gdn_branch(proj, small, small_t3, conv_w, expand, alog_row, dtb_row, alog_col, dtb_col, norm_g,
                bsz, seq, tt=2 * CHUNK):
    nc = seq // tt
    nch = tt // CHUNK
    t = bsz * seq
    full = lambda shape: pl.BlockSpec(shape, lambda b, c: (0,) * len(shape))
    return pl.pallas_call(
        _gdn_kernel,
        grid=(bsz, nc),
        in_specs=[
            pl.BlockSpec((tt, 3 * GDN_WIDTH), lambda b, c: (b * nc + c, COL_QKV)),
            pl.BlockSpec((tt, GDN_WIDTH), lambda b, c: (b * nc + c, COL_ZA)),
            pl.BlockSpec((tt, 128), lambda b, c: (b * nc + c, 0)),
            pl.BlockSpec((nch, 16, CHUNK), lambda b, c: (b * nc + c, 0, 0)),
            full((CONV_WIDTH, 3 * GDN_WIDTH)),
            full((128, 2 * GDN_WIDTH)),
            full((1, GDN_WIDTH)),
            full((1, GDN_WIDTH)),
            full((2 * N_HEADS, 1)),
            full((2 * N_HEADS, 1)),
            full((1, GDN_WIDTH)),
        ],
        out_specs=pl.BlockSpec((tt, GDN_WIDTH), lambda b, c: (b * nc + c, 0)),
        out_shape=jax.ShapeDtypeStruct((t, GDN_WIDTH), BF16),
        scratch_shapes=[
            pltpu.VMEM((tt + 8, 3 * GDN_WIDTH), F32),
            pltpu.VMEM((N_HEADS, HEAD_DIM, HEAD_DIM), F32),
        ],
        compiler_params=pltpu.CompilerParams(
            dimension_semantics=("parallel", "arbitrary"), vmem_limit_bytes=VMEM_LIMIT),
        name="gated_deltanet",
    )(proj, proj, small, small_t3, conv_w, expand, alog_row, dtb_row, alog_col, dtb_col, norm_g)


def _s5_kernel(xb_ref, zb_ref, bdb_ref, bdc_ref, tab_ref, d_ref, wglu_ref, o_ref,
               hre_scr, him_scr, carry_scr, y_scr):
    tt = xb_ref.shape[0]
    ns = S5_GROUPS * S5_STATE // S5_SLABS

    @pl.when(pl.program_id(1) == 0)
    def _():
        carry_scr[...] = jnp.zeros_like(carry_scr)

    x = xb_ref[...]
    for s in range(S5_SLABS):
        bu = jnp.dot(x[:, s * 256:(s + 1) * 256].astype(BF16), bdb_ref[s], preferred_element_type=F32)
        hre_scr[...] = bu[:, :ns]
        him_scr[...] = bu[:, ns:]

        def body(j, carry, s=s):
            cr, ci = carry
            r0 = pl.multiple_of(j * 8, 8)
            br = hre_scr[pl.ds(r0, 8), :]
            bi = him_scr[pl.ds(r0, 8), :]
            for lvl, dist in enumerate((1, 2, 4)):
                tr = tab_ref[s, lvl, 0]
                ti = tab_ref[s, lvl, 1]
                sr = pltpu.roll(br, dist, 0)
                si = pltpu.roll(bi, dist, 0)
                br, bi = br + (tr * sr - ti * si), bi + (tr * si + ti * sr)
            pr = tab_ref[s, 3, 0]
            pi = tab_ref[s, 3, 1]
            br, bi = br + (pr * cr - pi * ci), bi + (pr * ci + pi * cr)
            hre_scr[pl.ds(r0, 8), :] = br
            him_scr[pl.ds(r0, 8), :] = bi
            return (jnp.broadcast_to(br[7:8, :], (8, ns)), jnp.broadcast_to(bi[7:8, :], (8, ns)))

        cr, ci = lax.fori_loop(0, tt // 8, body, (carry_scr[s, 0], carry_scr[s, 1]))
        carry_scr[s, 0] = cr
        carry_scr[s, 1] = ci
        y_scr[:, s * 256:(s + 1) * 256] = (
            jnp.dot(hre_scr[...].astype(BF16), bdc_ref[s, :ns, :], preferred_element_type=F32)
            + jnp.dot(him_scr[...].astype(BF16), bdc_ref[s, ns:, :], preferred_element_type=F32))

    y = _gelu_tanh(y_scr[...] + d_ref[...] * x)
    glu = jnp.dot(y.astype(BF16), wglu_ref[...], preferred_element_type=F32)
    o_b = glu[:, :S5_WIDTH] * _sigmoid(glu[:, S5_WIDTH:])
    o_ref[...] = (o_b * _silu(zb_ref[...])).astype(o_ref.dtype)


def _s5_branch(proj, bdb, bdc, tabs, d_row, w_glu, bsz, seq, tt=256):
    nt = seq // tt
    ns = S5_GROUPS * S5_STATE // S5_SLABS
    full = lambda shape: pl.BlockSpec(shape, lambda b, i: (0,) * len(shape))
    return pl.pallas_call(
        _s5_kernel,
        grid=(bsz, nt),
        in_specs=[
            pl.BlockSpec((tt, S5_WIDTH), lambda b, i: (b * nt + i, COL_XB)),
            pl.BlockSpec((tt, S5_WIDTH), lambda b, i: (b * nt + i, COL_ZB)),
            full((S5_SLABS, 256, 2 * ns)),
            full((S5_SLABS, 2 * ns, 256)),
            full((S5_SLABS, 4, 2, 8, ns)),
            full((1, S5_WIDTH)),
            full((S5_WIDTH, 2 * S5_WIDTH)),
        ],
        out_specs=pl.BlockSpec((tt, S5_WIDTH), lambda b, i: (b * nt + i, 0)),
        out_shape=jax.ShapeDtypeStruct((bsz * seq, S5_WIDTH), BF16),
        scratch_shapes=[
            pltpu.VMEM((tt, ns), F32),
            pltpu.VMEM((tt, ns), F32),
            pltpu.VMEM((S5_SLABS, 2, 8, ns), F32),
            pltpu.VMEM((tt, S5_WIDTH), F32),
        ],
        compiler_params=pltpu.CompilerParams(
            dimension_semantics=("parallel", "arbitrary"), vmem_limit_bytes=VMEM_LIMIT),
        name="s5_layer",
    )(proj, proj, bdb, bdc, tabs, d_row, w_glu)


def _s5_tables(lam_re, lam_im, log_dt, b_re, b_im, c_re, c_im):
    g, p, gi = S5_GROUPS, S5_STATE, S5_GROUP
    gs = g // S5_SLABS
    lr, li = lam_re.astype(F32), lam_im.astype(F32)
    dt = jnp.exp(log_dt.astype(F32))[:, None]
    mag = jnp.exp(lr * dt)
    ab_re, ab_im = mag * jnp.cos(li * dt), mag * jnp.sin(li * dt)
    den = lr * lr + li * li
    nr, ni = ab_re - 1.0, ab_im
    coef_re = (nr * lr + ni * li) / den
    coef_im = (ni * lr - nr * li) / den
    br, bi = b_re.astype(F32), b_im.astype(F32)
    bb_re = coef_re[..., None] * br - coef_im[..., None] * bi
    bb_im = coef_re[..., None] * bi + coef_im[..., None] * br

    eye = jnp.eye(gs, dtype=F32)

    def blockdiag_in(bb):
        bb = bb.reshape(S5_SLABS, gs, p, gi)
        return jnp.einsum('sgpi,gh->sgihp', bb, eye).reshape(S5_SLABS, gs * gi, gs * p)

    def blockdiag_out(cc):
        cc = cc.reshape(S5_SLABS, gs, gi, p)
        return jnp.einsum('sgip,gh->sgphi', cc, eye).reshape(S5_SLABS, gs * p, gs * gi)

    bdb = jnp.concatenate([blockdiag_in(bb_re), blockdiag_in(bb_im)], axis=2).astype(BF16)
    bdc = jnp.concatenate([blockdiag_out(c_re.astype(F32)), -blockdiag_out(c_im.astype(F32))],
                          axis=1).astype(BF16)

    def cmul(a, b):
        return a[0] * b[0] - a[1] * b[1], a[0] * b[1] + a[1] * b[0]

    p1 = (ab_re.reshape(S5_SLABS, gs * p), ab_im.reshape(S5_SLABS, gs * p))
    p2 = cmul(p1, p1)
    p4 = cmul(p2, p2)
    rows = jnp.arange(8)[None, :, None]

    def masked(pw, dist):
        t = jnp.stack([pw[0], pw[1]], axis=1)[:, :, None, :]
        return jnp.where(rows[None] >= dist, jnp.broadcast_to(t, (S5_SLABS, 2, 8, gs * p)), 0.0)

    pows = [p1]
    for _ in range(7):
        pows.append(cmul(pows[-1], p1))
    ramp = jnp.stack([jnp.stack([pw[0] for pw in pows], axis=1),
                      jnp.stack([pw[1] for pw in pows], axis=1)], axis=1)
    tabs = jnp.stack([masked(p1, 1), masked(p2, 2), masked(p4, 4), ramp], axis=1)
    return bdb, bdc, tabs


def _kv_kernel(mem_ref, g_ref, w_ref, o_ref):
    m = mem_ref[0]
    ms = jnp.mean(m * m, axis=-1, keepdims=True)
    mn = ((m * lax.rsqrt(ms + EPS)) * g_ref[...]).astype(BF16)
    o_ref[0] = jnp.dot(mn, w_ref[...], preferred_element_type=F32).astype(o_ref.dtype)


def _memory_kv(mem, mem_norm_g, w_kv):
    bsz = mem.shape[0]
    return pl.pallas_call(
        _kv_kernel,
        grid=(bsz,),
        in_specs=[
            pl.BlockSpec((1, MEM_LEN, D_MODEL), lambda b: (b, 0, 0)),
            pl.BlockSpec((1, D_MODEL), lambda b: (0, 0)),
            pl.BlockSpec((D_MODEL, 2 * XA_WIDTH), lambda b: (0, 0)),
        ],
        out_specs=pl.BlockSpec((1, MEM_LEN, 2 * XA_WIDTH), lambda b: (b, 0, 0)),
        out_shape=jax.ShapeDtypeStruct((bsz, MEM_LEN, 2 * XA_WIDTH), BF16),
        compiler_params=pltpu.CompilerParams(
            dimension_semantics=("parallel",), vmem_limit_bytes=VMEM_LIMIT),
        name="memory_kv",
    )(mem, mem_norm_g, w_kv)


def _xattn_kernel(q_ref, zc_ref, kv_ref, o_ref):
    scale = XA_HEAD_DIM ** -0.5
    for h in range(XA_HEADS):
        lo = h * XA_HEAD_DIM
        q = q_ref[:, lo:lo + XA_HEAD_DIM].astype(BF16)
        k = kv_ref[0, :, lo:lo + XA_HEAD_DIM]
        v = kv_ref[0, :, XA_WIDTH + lo:XA_WIDTH + lo + XA_HEAD_DIM]
        s = lax.dot_general(q, k, (((1,), (1,)), ((), ())), preferred_element_type=F32) * scale
        e = jnp.exp(s - jnp.max(s, axis=-1, keepdims=True))
        p = e / jnp.sum(e, axis=-1, keepdims=True)
        o = jnp.dot(p.astype(BF16), v, preferred_element_type=F32)
        o_ref[:, lo:lo + XA_HEAD_DIM] = (o * _silu(zc_ref[:, lo:lo + XA_HEAD_DIM])).astype(o_ref.dtype)


def _xattn_branch(proj, kv, bsz, seq, tt=512):
    nt = seq // tt
    return pl.pallas_call(
        _xattn_kernel,
        grid=(bsz, nt),
        in_specs=[
            pl.BlockSpec((tt, XA_WIDTH), lambda b, i: (b * nt + i, COL_QC)),
            pl.BlockSpec((tt, XA_WIDTH), lambda b, i: (b * nt + i, COL_ZC)),
            pl.BlockSpec((1, MEM_LEN, 2 * XA_WIDTH), lambda b, i: (b, 0, 0)),
        ],
        out_specs=pl.BlockSpec((tt, XA_WIDTH), lambda b, i: (b * nt + i, 0)),
        out_shape=jax.ShapeDtypeStruct((bsz * seq, XA_WIDTH), BF16),
        compiler_params=pltpu.CompilerParams(
            dimension_semantics=("parallel", "parallel"), vmem_limit_bytes=VMEM_LIMIT),
        name="memory_xattn",
    )(proj, proj, kv)


def _merge_kernel(a_ref, b_ref, c_ref, g0_ref, g1_ref, g2_ref, x_ref, wa_ref, wb_ref, wc_ref,
                  wo_ref, fg_ref, o_ref):
    merged = _sigmoid(g0_ref[...]) * jnp.dot(a_ref[...], wa_ref[...], preferred_element_type=F32)
    merged = merged + _sigmoid(g1_ref[...]) * jnp.dot(b_ref[...], wb_ref[...], preferred_element_type=F32)
    merged = merged + _sigmoid(g2_ref[...]) * jnp.dot(c_ref[...], wc_ref[...], preferred_element_type=F32)
    h = x_ref[...] + jnp.dot(merged.astype(BF16), wo_ref[...], preferred_element_type=F32)
    ms = jnp.mean(h * h, axis=-1, keepdims=True)
    o_ref[...] = (h * lax.rsqrt(ms + EPS)) * fg_ref[...]


def _merge_output(a, b, c, proj, x2d, wa, wb, wc, wo, final_g, tm=256):
    t = x2d.shape[0]
    gb = COL_GATES // 2
    const = lambda shape: pl.BlockSpec(shape, lambda i: (0,) * len(shape), pipeline_mode=pl.Buffered(1))
    return pl.pallas_call(
        _merge_kernel,
        grid=(t // tm,),
        in_specs=[
            pl.BlockSpec((tm, GDN_WIDTH), lambda i: (i, 0)),
            pl.BlockSpec((tm, S5_WIDTH), lambda i: (i, 0)),
            pl.BlockSpec((tm, XA_WIDTH), lambda i: (i, 0)),
            pl.BlockSpec((tm, D_MODEL), lambda i: (i, gb)),
            pl.BlockSpec((tm, D_MODEL), lambda i: (i, gb + 1)),
            pl.BlockSpec((tm, D_MODEL), lambda i: (i, gb + 2)),
            pl.BlockSpec((tm, D_MODEL), lambda i: (i, 0)),
            const((GDN_WIDTH, D_MODEL)),
            const((S5_WIDTH, D_MODEL)),
            const((XA_WIDTH, D_MODEL)),
            const((D_MODEL, D_MODEL)),
            const((1, D_MODEL)),
        ],
        out_specs=pl.BlockSpec((tm, D_MODEL), lambda i: (i, 0)),
        out_shape=jax.ShapeDtypeStruct((t, D_MODEL), F32),
        compiler_params=pltpu.CompilerParams(
            dimension_semantics=("parallel",), vmem_limit_bytes=VMEM_LIMIT),
        name="merge_output",
    )(a, b, c, proj, proj, proj, x2d, wa, wb, wc, wo, final_g)


def kernel(x, mem, norm_g, w_in, conv_w, gdn_a_log, gdn_dt_bias, gdn_norm_g, s5_lambda_re, s5_lambda_im,
           s5_log_dt, s5_b_re, s5_b_im, s5_c_re, s5_c_im, s5_d, s5_w_glu, mem_norm_g, w_kv_mem, w_br_a,
           w_br_b, w_br_c, w_out, final_g):
    bsz, seq, d = x.shape
    t = bsz * seq
    h = x.reshape(t, d)
    assert norm_g.shape[0] == 1, "the merge kernel fuses the final RMSNorm: single layer only"
    for l in range(1):
        wl = w_in[l]
        n_a = 4 * GDN_WIDTH
        w_main = jnp.concatenate([wl[:, :n_a], wl[:, n_a + 2 * N_HEADS:]], axis=1).astype(BF16)
        w_small = wl[:, n_a:n_a + 2 * N_HEADS]
        w_small_t = w_small.T.astype(BF16)
        w_small = jnp.pad(w_small, ((0, 0), (0, 128 - 2 * N_HEADS))).astype(BF16)

        proj, small, small_t = _in_projection(h, norm_g[l][None, :], w_main, w_small, w_small_t)

        small_t3 = small_t.reshape(2 * N_HEADS, t // CHUNK, CHUNK).transpose(1, 0, 2)
        expand = (jnp.arange(2 * GDN_WIDTH)[None, :] // HEAD_DIM == jnp.arange(128)[:, None]).astype(BF16)
        alog = gdn_a_log[l].astype(F32)
        dtb = gdn_dt_bias[l].astype(F32)
        o_a = _gdn_branch(proj, small, small_t3, conv_w[l], expand,
                          jnp.repeat(alog, HEAD_DIM)[None, :], jnp.repeat(dtb, HEAD_DIM)[None, :],
                          jnp.pad(alog, (N_HEADS, 0))[:, None], jnp.pad(dtb, (N_HEADS, 0))[:, None],
                          jnp.tile(gdn_norm_g[l].astype(F32), N_HEADS)[None, :], bsz, seq)

        bdb, bdc, tabs = _s5_tables(s5_lambda_re[l], s5_lambda_im[l], s5_log_dt[l], s5_b_re[l], s5_b_im[l],
                                    s5_c_re[l], s5_c_im[l])
        o_b = _s5_branch(proj, bdb, bdc, tabs, s5_d[l][None, :], s5_w_glu[l].astype(BF16), bsz, seq)

        kv = _memory_kv(mem, mem_norm_g[l][None, :], w_kv_mem[l].astype(BF16))
        o_c = _xattn_branch(proj, kv, bsz, seq)

        out = _merge_output(o_a, o_b, o_c, proj, h, w_br_a[l].astype(BF16), w_br_b[l].astype(BF16),
                            w_br_c[l].astype(BF16), w_out[l].astype(BF16), final_g[None, :])
    return out.reshape(bsz, seq, d)
```

```python
import functools
import math

import jax
import jax.numpy as jnp
from jax import lax
from jax.experimental import pallas as pl
from jax.experimental.pallas import tpu as pltpu

F32 = jnp.float32
BF16 = jnp.bfloat16
EPS = 1e-6

D_MODEL = 2048
CHUNK = 64
N_HEADS = 8
HEAD_DIM = 128
GDN_WIDTH = N_HEADS * HEAD_DIM
CONV_WIDTH = 4
S5_WIDTH = 1024
S5_GROUP = 16
S5_GROUPS = 64
S5_STATE = 64
S5_SLABS = 4
S5_TILE = 256
XA_HEADS = 4
XA_HEAD_DIM = 256
XA_WIDTH = 1024
MEM_LEN = 256
VMEM_LIMIT = 56 * 1024 * 1024

COL_QKV = 0
COL_ZA = 3
COL_XB = 4
COL_ZB = 5
COL_QC = 6
COL_ZC = 7
COL_GATES = 8
N_MAIN = 14 * 1024


def _mm(a, b):
    return jnp.dot(a.astype(BF16), b.astype(BF16), preferred_element_type=F32)


def _mm_nt(a, b):
    return lax.dot_general(a.astype(BF16), b.astype(BF16), (((1,), (1,)), ((), ())),
                           preferred_element_type=F32)


def _mm_tn(a, b):
    return lax.dot_general(a.astype(BF16), b.astype(BF16), (((0,), (0,)), ((), ())),
                           preferred_element_type=F32)


def _split3(x):
    x0 = x.astype(BF16)
    r = x - x0.astype(F32)
    x1 = r.astype(BF16)
    r = r - x1.astype(F32)
    return x0, x1, r.astype(BF16)


def _sel_rhs(x, sel):
    x0, x1, x2 = _split3(x)
    d = lambda t: jnp.dot(t, sel, preferred_element_type=F32)
    return d(x0) + d(x1) + d(x2)


def _sel_lhs(sel, x):
    x0, x1, x2 = _split3(x)
    d = lambda t: jnp.dot(sel, t, preferred_element_type=F32)
    return d(x0) + d(x1) + d(x2)


def _sigmoid(x):
    return 1.0 / (1.0 + jnp.exp(-x))


def _silu(x):
    return x * _sigmoid(x)


def _softplus(x):
    return jnp.maximum(x, 0.0) + jnp.log1p(jnp.exp(-jnp.abs(x)))


def _gelu_tanh(x):
    return 0.5 * x * (1.0 + jnp.tanh(math.sqrt(2.0 / math.pi) * (x + 0.044715 * (x * x * x))))


def _inproj_kernel(x_ref, g_ref, wa_ref, wb_ref, ws_ref, wst_ref, o_ref, os_ref, ost_ref, u_scr, *, na):
    j = pl.program_id(1)

    @pl.when(j == 0)
    def _():
        x = x_ref[...]
        ms = jnp.mean(x * x, axis=-1, keepdims=True)
        u = ((x * lax.rsqrt(ms + EPS)) * g_ref[...]).astype(BF16)
        u_scr[...] = u
        os_ref[...] = jnp.dot(u, ws_ref[...], preferred_element_type=F32)
        ost_ref[...] = lax.dot_general(wst_ref[...], u, (((1,), (1,)), ((), ())),
                                       preferred_element_type=F32)

    @pl.when(j < na)
    def _():
        o_ref[...] = jnp.dot(u_scr[...], wa_ref[...], preferred_element_type=F32)

    @pl.when(j >= na)
    def _():
        o_ref[...] = jnp.dot(u_scr[...], wb_ref[...], preferred_element_type=F32)


def _in_projection(x2d, norm_g, w_a, w_b, w_small, w_small_t, tm=1024, tn=1024):
    t, d = x2d.shape
    na = w_a.shape[1] // tn
    n = w_a.shape[1] + w_b.shape[1]
    return pl.pallas_call(
        functools.partial(_inproj_kernel, na=na),
        grid=(t // tm, n // tn),
        in_specs=[
            pl.BlockSpec((tm, d), lambda i, j: (i, 0)),
            pl.BlockSpec((1, d), lambda i, j: (0, 0)),
            pl.BlockSpec((d, tn), lambda i, j: (0, jnp.minimum(j, na - 1))),
            pl.BlockSpec((d, tn), lambda i, j: (0, jnp.maximum(j - na, 0))),
            pl.BlockSpec((d, 128), lambda i, j: (0, 0)),
            pl.BlockSpec((16, d), lambda i, j: (0, 0)),
        ],
        out_specs=[
            pl.BlockSpec((tm, tn), lambda i, j: (i, j)),
            pl.BlockSpec((tm, 128), lambda i, j: (i, 0)),
            pl.BlockSpec((16, tm), lambda i, j: (0, i)),
        ],
        out_shape=[
            jax.ShapeDtypeStruct((t, n), F32),
            jax.ShapeDtypeStruct((t, 128), F32),
            jax.ShapeDtypeStruct((16, t), F32),
        ],
        scratch_shapes=[pltpu.VMEM((tm, d), BF16)],
        compiler_params=pltpu.CompilerParams(
            dimension_semantics=("parallel", "arbitrary"), vmem_limit_bytes=VMEM_LIMIT),
        name="in_projection",
    )(x2d, norm_g, w_a, w_b, w_small, w_small_t)


def _each(f, *lists):
    return [f(*args) for args in zip(*lists)]


def _unit_lower_inverse(lows, eye, blk16, blk32):
    n1 = _each(lambda l: jnp.where(blk16, -l, 0.0), lows)
    n2 = _each(_mm, n1, n1)
    n4 = _each(_mm, n2, n2)
    n3 = _each(_mm, n1, n2)
    n8 = _each(_mm, n4, n4)
    p = _each(lambda a, b, d: eye + a + b + d, n1, n2, n3)
    p = _each(lambda a, b: a + _mm(a, b), p, n4)
    p = _each(lambda a, b: a + _mm(a, b), p, n8)
    for mask in (jnp.logical_and(blk32, jnp.logical_not(blk16)), jnp.logical_not(blk32)):
        pe = _each(lambda a, l: _mm(a, jnp.where(mask, l, 0.0)), p, lows)
        p = _each(lambda a, b: a - _mm(b, a), p, pe)
    return p


def _gdn_kernel(qkv_ref, za_ref, sm_ref, smt_ref, convw_ref, expand_ref, alog_row_ref, dtb_row_ref,
                alog_col_ref, dtb_col_ref, ng_ref, o_ref, ext_scr, state_scr):
    c = CHUNK
    tt = qkv_ref.shape[0]
    nch = tt // c

    @pl.when(pl.program_id(1) == 0)
    def _():
        ext_scr[0:8, :] = jnp.zeros((8, 3 * GDN_WIDTH), F32)
        state_scr[...] = jnp.zeros_like(state_scr)

    raw = qkv_ref[...]
    ext_scr[8:8 + tt, :] = raw
    conv = convw_ref[3:4, :] * raw
    for j in range(CONV_WIDTH - 1):
        conv = conv + convw_ref[j:j + 1, :] * ext_scr[5 + j:5 + j + tt, :]
    ext_scr[0:8, :] = raw[tt - 8:tt, :]
    qkv = _silu(conv)

    row = lax.broadcasted_iota(jnp.int32, (c, c), 0)
    col = lax.broadcasted_iota(jnp.int32, (c, c), 1)
    incl = row >= col
    strict = row > col
    blk16 = (row >> 4) == (col >> 4)
    blk32 = (row >> 5) == (col >> 5)
    eye = jnp.where(row == col, 1.0, 0.0).astype(F32)
    tri_up = jnp.where(row <= col, 1.0, 0.0).astype(BF16)
    rt = lax.broadcasted_iota(jnp.int32, (tt, tt), 0)
    ct = lax.broadcasted_iota(jnp.int32, (tt, tt), 1)
    tri_tile = jnp.where(jnp.logical_and(rt >= ct, (rt >> 6) == (ct >> 6)), 1.0, 0.0).astype(BF16)

    rawb = _sel_rhs(sm_ref[...], expand_ref[...])
    beta_b = _sigmoid(rawb[:, :GDN_WIDTH])
    g_b = -jnp.exp(alog_row_ref[...]) * _softplus(rawb[:, GDN_WIDTH:] + dtb_row_ref[...])
    gc_b = _sel_lhs(tri_tile, g_b)
    egc_b = jnp.exp(gc_b)
    gc_r = []
    for ci in range(nch):
        g_r = -jnp.exp(alog_col_ref[...]) * _softplus(smt_ref[ci] + dtb_col_ref[...])
        gc_r.append(_sel_rhs(g_r, tri_up)[N_HEADS:, :])

    chains = [(ci, h) for ci in range(nch) for h in range(N_HEADS)]

    def part(arr, ci, h, base=0):
        return arr[ci * c:(ci + 1) * c, base + h * HEAD_DIM:base + (h + 1) * HEAD_DIM]

    def l2n(a):
        return a * lax.rsqrt(jnp.sum(a * a, axis=-1, keepdims=True) + EPS)

    q = [l2n(part(qkv, ci, h)) * (HEAD_DIM ** -0.5) for ci, h in chains]
    k = [l2n(part(qkv, ci, h, GDN_WIDTH)) for ci, h in chains]
    v = [part(qkv, ci, h, 2 * GDN_WIDTH) for ci, h in chains]
    beta = [part(beta_b, ci, h) for ci, h in chains]
    gcb = [part(gc_b, ci, h) for ci, h in chains]
    egc = [part(egc_b, ci, h) for ci, h in chains]
    kb = _each(lambda a, b: a * b, k, beta)
    aq = _each(lambda a, b, d: _mm_nt(jnp.concatenate([a, b], axis=0), d), kb, q, k)
    decay = [jnp.exp(jnp.where(incl, g[:, :c] - gc_r[ci][h:h + 1, :], -jnp.inf))
             for g, (ci, h) in zip(gcb, chains)]
    low = _each(lambda a, d: jnp.where(strict, a[:c] * d, 0.0), aq, decay)
    tinv = _unit_lower_inverse(low, eye, blk16, blk32)
    sol = _each(lambda t, a, b, d, e: _mm(t, jnp.concatenate([a * b, d * e], axis=1)),
                tinv, v, beta, kb, egc)
    wq = _each(lambda s, a, e: jnp.concatenate([s[:, HEAD_DIM:], a * e], axis=0), sol, q, egc)
    qkd = _each(lambda a, d: a[c:] * d, aq, decay)
    k_dec = _each(lambda a, g: a * jnp.exp(g[c - 1:c, :] - g), k, gcb)
    g_last = [jnp.exp(g[c - 1:c, :]) for g in gcb]

    states = [state_scr[h] for h in range(N_HEADS)]
    outs = []
    for ci in range(nch):
        idx = [ci * N_HEADS + h for h in range(N_HEADS)]
        ws = [_mm(wq[i], s) for i, s in zip(idx, states)]
        v_new = [sol[i][:, :HEAD_DIM] - w[:c] for i, w in zip(idx, ws)]
        o = [w[c:] + _mm(qkd[i], vn) for i, w, vn in zip(idx, ws, v_new)]
        states = [s * g_last[i] + _mm_tn(k_dec[i], vn) for i, s, vn in zip(idx, states, v_new)]
        o = [a * lax.rsqrt(jnp.mean(a * a, axis=-1, keepdims=True) + EPS) for a in o]
        outs.append(jnp.concatenate(o, axis=1))
    o_tile = jnp.concatenate(outs, axis=0) if nch > 1 else outs[0]
    o_ref[...] = ((o_tile * ng_ref[...]) * _silu(za_ref[...])).astype(o_ref.dtype)
    for h in range(N_HEADS):
        state_scr[h] = states[h]


def _gdn_branch(proj, small, small_t3, conv_w, expand, alog_row, dtb_row, alog_col, dtb_col, norm_g,
                bsz, seq, tt=2 * CHUNK):
    nc = seq // tt
    nch = tt // CHUNK
    t = bsz * seq
    full = lambda shape: pl.BlockSpec(shape, lambda b, c: (0,) * len(shape))
    return pl.pallas_call(
        _gdn_kernel,
        grid=(bsz, nc),
        in_specs=[
            pl.BlockSpec((tt, 3 * GDN_WIDTH), lambda b, c: (b * nc + c, COL_QKV)),
            pl.BlockSpec((tt, GDN_WIDTH), lambda b, c: (b * nc + c, COL_ZA)),
            pl.BlockSpec((tt, 128), lambda b, c: (b * nc + c, 0)),
            pl.BlockSpec((nch, 16, CHUNK), lambda b, c: (b * nc + c, 0, 0)),
            full((CONV_WIDTH, 3 * GDN_WIDTH)),
            full((128, 2 * GDN_WIDTH)),
            full((1, GDN_WIDTH)),
            full((1, GDN_WIDTH)),
            full((2 * N_HEADS, 1)),
            full((2 * N_HEADS, 1)),
            full((1, GDN_WIDTH)),
        ],
        out_specs=pl.BlockSpec((tt, GDN_WIDTH), lambda b, c: (b * nc + c, 0)),
        out_shape=jax.ShapeDtypeStruct((t, GDN_WIDTH), BF16),
        scratch_shapes=[
            pltpu.VMEM((tt + 8, 3 * GDN_WIDTH), F32),
            pltpu.VMEM((N_HEADS, HEAD_DIM, HEAD_DIM), F32),
        ],
        compiler_params=pltpu.CompilerParams(
            dimension_semantics=("parallel", "arbitrary"), vmem_limit_bytes=VMEM_LIMIT),
        name="gated_deltanet",
    )(proj, proj, small, small_t3, conv_w, expand, alog_row, dtb_row, alog_col, dtb_col, norm_g)


def _s5_kernel(xb_ref, zb_ref, perm_ref, permt_ref, bdb_ref, bdc_ref, tab_ref, d_ref, wglu_ref, o_ref,
               h_scr, carry_scr):
    tt = xb_ref.shape[0]
    nj = tt // 8
    ns = S5_GROUPS * S5_STATE // S5_SLABS

    @pl.when(pl.program_id(1) == 0)
    def _():
        carry_scr[...] = jnp.zeros_like(carry_scr)

    def cmul(ar, ai, br, bi):
        return ar * br - ai * bi, ar * bi + ai * br

    perm = perm_ref[...]
    x0, x1, x2 = _split3(xb_ref[...])
    pdot = lambda t: jnp.dot(perm, t, preferred_element_type=F32)
    xp0 = pdot(x0)
    xp = xp0 + pdot(x1) + pdot(x2)
    xpb = xp0.astype(BF16)
    not_first = lax.broadcasted_iota(jnp.int32, (8, ns), 0) >= 1

    ys = []
    for s in range(S5_SLABS):
        h_scr[s] = jnp.dot(xpb[:, s * 256:(s + 1) * 256], bdb_ref[s], preferred_element_type=F32)
        ar, ai = tab_ref[s, 0, 0], tab_ref[s, 0, 1]
        er = h_scr[s, 0:8, :ns]
        ei = h_scr[s, 0:8, ns:]
        for j in range(1, nj):
            mr, mi = cmul(ar, ai, er, ei)
            er = mr + h_scr[s, j * 8:(j + 1) * 8, :ns]
            ei = mi + h_scr[s, j * 8:(j + 1) * 8, ns:]
        for lvl, dist in enumerate((1, 2, 4)):
            mr, mi = cmul(tab_ref[s, 1 + lvl, 0], tab_ref[s, 1 + lvl, 1],
                          pltpu.roll(er, dist, 0), pltpu.roll(ei, dist, 0))
            er, ei = er + mr, ei + mi
        cr, ci = carry_scr[s, 0], carry_scr[s, 1]
        mr, mi = cmul(tab_ref[s, 4, 0], tab_ref[s, 4, 1], cr, ci)
        gr = jnp.where(not_first, pltpu.roll(er, 1, 0), 0.0) + mr
        gi = jnp.where(not_first, pltpu.roll(ei, 1, 0), 0.0) + mi
        mr, mi = cmul(tab_ref[s, 5, 0], tab_ref[s, 5, 1], cr, ci)
        carry_scr[s, 0] = jnp.broadcast_to(er[7:8, :], (8, ns)) + mr
        carry_scr[s, 1] = jnp.broadcast_to(ei[7:8, :], (8, ns)) + mi
        for j in range(nj):
            mr, mi = cmul(ar, ai, gr, gi)
            gr = mr + h_scr[s, j * 8:(j + 1) * 8, :ns]
            gi = mi + h_scr[s, j * 8:(j + 1) * 8, ns:]
            h_scr[s, j * 8:(j + 1) * 8, :ns] = gr
            h_scr[s, j * 8:(j + 1) * 8, ns:] = gi
        ys.append(jnp.dot(h_scr[s, :, :ns].astype(BF16), bdc_ref[s, :ns, :], preferred_element_type=F32)
                  + jnp.dot(h_scr[s, :, ns:].astype(BF16), bdc_ref[s, ns:, :], preferred_element_type=F32))

    y = _gelu_tanh(jnp.concatenate(ys, axis=1) + d_ref[...] * xp)
    glu = jnp.dot(y.astype(BF16), wglu_ref[...], preferred_element_type=F32)
    o_b = (glu[:, :S5_WIDTH] * _sigmoid(glu[:, S5_WIDTH:])).astype(BF16)
    o_b = jnp.dot(permt_ref[...], o_b, preferred_element_type=F32)
    o_ref[...] = (o_b * _silu(zb_ref[...])).astype(o_ref.dtype)


def _s5_branch(proj, bdb, bdc, tabs, d_row, w_glu, bsz, seq, tt=256):
    nt = seq // tt
    ns = S5_GROUPS * S5_STATE // S5_SLABS
    rho = jnp.arange(tt)
    perm = ((rho % 8) * (tt // 8) + rho // 8)[:, None] == jnp.arange(tt)[None, :]
    full = lambda shape: pl.BlockSpec(shape, lambda b, i: (0,) * len(shape))
    return pl.pallas_call(
        _s5_kernel,
        grid=(bsz, nt),
        in_specs=[
            pl.BlockSpec((tt, S5_WIDTH), lambda b, i: (b * nt + i, COL_XB)),
            pl.BlockSpec((tt, S5_WIDTH), lambda b, i: (b * nt + i, COL_ZB)),
            full((tt, tt)),
            full((tt, tt)),
            full((S5_SLABS, 256, 2 * ns)),
            full((S5_SLABS, 2 * ns, 256)),
            full((S5_SLABS, 6, 2, 8, ns)),
            full((1, S5_WIDTH)),
            full((S5_WIDTH, 2 * S5_WIDTH)),
        ],
        out_specs=pl.BlockSpec((tt, S5_WIDTH), lambda b, i: (b * nt + i, 0)),
        out_shape=jax.ShapeDtypeStruct((bsz * seq, S5_WIDTH), BF16),
        scratch_shapes=[
            pltpu.VMEM((S5_SLABS, tt, 2 * ns), F32),
            pltpu.VMEM((S5_SLABS, 2, 8, ns), F32),
        ],
        compiler_params=pltpu.CompilerParams(
            dimension_semantics=("parallel", "arbitrary"), vmem_limit_bytes=VMEM_LIMIT),
        name="s5_layer",
    )(proj, proj, perm.astype(BF16), perm.T.astype(BF16), bdb, bdc, tabs, d_row, w_glu)


def _s5_tables(lam_re, lam_im, log_dt, b_re, b_im, c_re, c_im, seg):
    g, p, gi = S5_GROUPS, S5_STATE, S5_GROUP
    gs = g // S5_SLABS
    lr, li = lam_re.astype(F32), lam_im.astype(F32)
    dt = jnp.exp(log_dt.astype(F32))[:, None]
    mag = jnp.exp(lr * dt)
    ab_re, ab_im = mag * jnp.cos(li * dt), mag * jnp.sin(li * dt)
    den = lr * lr + li * li
    nr, ni = ab_re - 1.0, ab_im
    coef_re = (nr * lr + ni * li) / den
    coef_im = (ni * lr - nr * li) / den
    br, bi = b_re.astype(F32), b_im.astype(F32)
    bb_re = coef_re[..., None] * br - coef_im[..., None] * bi
    bb_im = coef_re[..., None] * bi + coef_im[..., None] * br

    eye = jnp.eye(gs, dtype=F32)

    def blockdiag_in(bb):
        bb = bb.reshape(S5_SLABS, gs, p, gi)
        return jnp.einsum('sgpi,gh->sgihp', bb, eye).reshape(S5_SLABS, gs * gi, gs * p)

    def blockdiag_out(cc):
        cc = cc.reshape(S5_SLABS, gs, gi, p)
        return jnp.einsum('sgip,gh->sgphi', cc, eye).reshape(S5_SLABS, gs * p, gs * gi)

    bdb = jnp.concatenate([blockdiag_in(bb_re), blockdiag_in(bb_im)], axis=2).astype(BF16)
    bdc = jnp.concatenate([blockdiag_out(c_re.astype(F32)), -blockdiag_out(c_im.astype(F32))],
                          axis=1).astype(BF16)

    def cmul(a, b):
        return a[0] * b[0] - a[1] * b[1], a[0] * b[1] + a[1] * b[0]

    p1 = (ab_re.reshape(S5_SLABS, gs * p), ab_im.reshape(S5_SLABS, gs * p))
    pseg = p1
    for _ in range(int(math.log2(seg))):
        pseg = cmul(pseg, pseg)
    pseg2 = cmul(pseg, pseg)
    pseg4 = cmul(pseg2, pseg2)
    ptile = cmul(pseg4, pseg4)
    rows = jnp.arange(8)[None, None, :, None]

    def bcast(pw):
        return jnp.broadcast_to(jnp.stack([pw[0], pw[1]], axis=1)[:, :, None, :], (S5_SLABS, 2, 8, gs * p))

    def masked(pw, dist):
        return jnp.where(rows >= dist, bcast(pw), 0.0)

    pows = [(jnp.ones_like(p1[0]), jnp.zeros_like(p1[0]))]
    for _ in range(7):
        pows.append(cmul(pows[-1], pseg))
    ramp = jnp.stack([jnp.stack([pw[0] for pw in pows], axis=1),
                      jnp.stack([pw[1] for pw in pows], axis=1)], axis=1)
    tabs = jnp.stack([bcast(p1), masked(pseg, 1), masked(pseg2, 2), masked(pseg4, 4), ramp, bcast(ptile)],
                     axis=1)
    return bdb, bdc, tabs


def _kv_kernel(mem_ref, g_ref, w_ref, o_ref):
    m = mem_ref[0]
    ms = jnp.mean(m * m, axis=-1, keepdims=True)
    mn = ((m * lax.rsqrt(ms + EPS)) * g_ref[...]).astype(BF16)
    o_ref[0] = jnp.dot(mn, w_ref[...], preferred_element_type=F32).astype(o_ref.dtype)


def _memory_kv(mem, mem_norm_g, w_kv):
    bsz = mem.shape[0]
    return pl.pallas_call(
        _kv_kernel,
        grid=(bsz,),
        in_specs=[
            pl.BlockSpec((1, MEM_LEN, D_MODEL), lambda b: (b, 0, 0)),
            pl.BlockSpec((1, D_MODEL), lambda b: (0, 0)),
            pl.BlockSpec((D_MODEL, 2 * XA_WIDTH), lambda b: (0, 0)),
        ],
        out_specs=pl.BlockSpec((1, MEM_LEN, 2 * XA_WIDTH), lambda b: (b, 0, 0)),
        out_shape=jax.ShapeDtypeStruct((bsz, MEM_LEN, 2 * XA_WIDTH), BF16),
        compiler_params=pltpu.CompilerParams(
            dimension_semantics=("parallel",), vmem_limit_bytes=VMEM_LIMIT),
        name="memory_kv",
    )(mem, mem_norm_g, w_kv)


def _xattn_kernel(q_ref, zc_ref, kv_ref, o_ref):
    scale = XA_HEAD_DIM ** -0.5
    for h in range(XA_HEADS):
        lo = h * XA_HEAD_DIM
        q = q_ref[:, lo:lo + XA_HEAD_DIM].astype(BF16)
        k = kv_ref[0, :, lo:lo + XA_HEAD_DIM]
        v = kv_ref[0, :, XA_WIDTH + lo:XA_WIDTH + lo + XA_HEAD_DIM]
        s = lax.dot_general(q, k, (((1,), (1,)), ((), ())), preferred_element_type=F32) * scale
        e = jnp.exp(s - jnp.max(s, axis=-1, keepdims=True))
        p = e / jnp.sum(e, axis=-1, keepdims=True)
        o = jnp.dot(p.astype(BF16), v, preferred_element_type=F32)
        o_ref[:, lo:lo + XA_HEAD_DIM] = (o * _silu(zc_ref[:, lo:lo + XA_HEAD_DIM])).astype(o_ref.dtype)


def _xattn_branch(proj, kv, bsz, seq, tt=512):
    nt = seq // tt
    return pl.pallas_call(
        _xattn_kernel,
        grid=(bsz, nt),
        in_specs=[
            pl.BlockSpec((tt, XA_WIDTH), lambda b, i: (b * nt + i, COL_QC)),
            pl.BlockSpec((tt, XA_WIDTH), lambda b, i: (b * nt + i, COL_ZC)),
            pl.BlockSpec((1, MEM_LEN, 2 * XA_WIDTH), lambda b, i: (b, 0, 0)),
        ],
        out_specs=pl.BlockSpec((tt, XA_WIDTH), lambda b, i: (b * nt + i, 0)),
        out_shape=jax.ShapeDtypeStruct((bsz * seq, XA_WIDTH), BF16),
        compiler_params=pltpu.CompilerParams(
            dimension_semantics=("parallel", "parallel"), vmem_limit_bytes=VMEM_LIMIT),
        name="memory_xattn",
    )(proj, proj, kv)


def _merge_kernel(a_ref, b_ref, c_ref, g0_ref, g1_ref, g2_ref, x_ref, wa_ref, wb_ref, wc_ref,
                  wo_ref, fg_ref, o_ref):
    merged = _sigmoid(g0_ref[...]) * jnp.dot(a_ref[...], wa_ref[...], preferred_element_type=F32)
    merged = merged + _sigmoid(g1_ref[...]) * jnp.dot(b_ref[...], wb_ref[...], preferred_element_type=F32)
    merged = merged + _sigmoid(g2_ref[...]) * jnp.dot(c_ref[...], wc_ref[...], preferred_element_type=F32)
    h = x_ref[...] + jnp.dot(merged.astype(BF16), wo_ref[...], preferred_element_type=F32)
    ms = jnp.mean(h * h, axis=-1, keepdims=True)
    o_ref[...] = (h * lax.rsqrt(ms + EPS)) * fg_ref[...]


def _merge_output(a, b, c, proj, x2d, wa, wb, wc, wo, final_g, tm=256):
    t = x2d.shape[0]
    gb = COL_GATES // 2
    const = lambda shape: pl.BlockSpec(shape, lambda i: (0,) * len(shape), pipeline_mode=pl.Buffered(1))
    return pl.pallas_call(
        _merge_kernel,
        grid=(t // tm,),
        in_specs=[
            pl.BlockSpec((tm, GDN_WIDTH), lambda i: (i, 0)),
            pl.BlockSpec((tm, S5_WIDTH), lambda i: (i, 0)),
            pl.BlockSpec((tm, XA_WIDTH), lambda i: (i, 0)),
            pl.BlockSpec((tm, D_MODEL), lambda i: (i, gb)),
            pl.BlockSpec((tm, D_MODEL), lambda i: (i, gb + 1)),
            pl.BlockSpec((tm, D_MODEL), lambda i: (i, gb + 2)),
            pl.BlockSpec((tm, D_MODEL), lambda i: (i, 0)),
            const((GDN_WIDTH, D_MODEL)),
            const((S5_WIDTH, D_MODEL)),
            const((XA_WIDTH, D_MODEL)),
            const((D_MODEL, D_MODEL)),
            const((1, D_MODEL)),
        ],
        out_specs=pl.BlockSpec((tm, D_MODEL), lambda i: (i, 0)),
        out_shape=jax.ShapeDtypeStruct((t, D_MODEL), F32),
        compiler_params=pltpu.CompilerParams(
            dimension_semantics=("parallel",), vmem_limit_bytes=VMEM_LIMIT),
        name="merge_output",
    )(a, b, c, proj, proj, proj, x2d, wa, wb, wc, wo, final_g)


def kernel(x, mem, norm_g, w_in, conv_w, gdn_a_log, gdn_dt_bias, gdn_norm_g, s5_lambda_re, s5_lambda_im,
           s5_log_dt, s5_b_re, s5_b_im, s5_c_re, s5_c_im, s5_d, s5_w_glu, mem_norm_g, w_kv_mem, w_br_a,
           w_br_b, w_br_c, w_out, final_g):
    bsz, seq, d = x.shape
    t = bsz * seq
    h = x.reshape(t, d)
    assert norm_g.shape[0] == 1, "the merge kernel fuses the final RMSNorm: single layer only"
    for l in range(1):
        wl = w_in[l]
        n_a = 4 * GDN_WIDTH
        w_a = wl[:, :n_a].astype(BF16)
        w_b = wl[:, n_a + 2 * N_HEADS:].astype(BF16)
        w_small = wl[:, n_a:n_a + 2 * N_HEADS]
        w_small_t = w_small.T.astype(BF16)
        w_small = jnp.pad(w_small, ((0, 0), (0, 128 - 2 * N_HEADS))).astype(BF16)

        proj, small, small_t = _in_projection(h, norm_g[l][None, :], w_a, w_b, w_small, w_small_t)

        small_t3 = small_t.reshape(2 * N_HEADS, t // CHUNK, CHUNK).transpose(1, 0, 2)
        expand = (jnp.arange(2 * GDN_WIDTH)[None, :] // HEAD_DIM == jnp.arange(128)[:, None]).astype(BF16)
        alog = gdn_a_log[l].astype(F32)
        dtb = gdn_dt_bias[l].astype(F32)
        o_a = _gdn_branch(proj, small, small_t3, conv_w[l], expand,
                          jnp.repeat(alog, HEAD_DIM)[None, :], jnp.repeat(dtb, HEAD_DIM)[None, :],
                          jnp.pad(alog, (N_HEADS, 0))[:, None], jnp.pad(dtb, (N_HEADS, 0))[:, None],
                          jnp.tile(gdn_norm_g[l].astype(F32), N_HEADS)[None, :], bsz, seq)

        bdb, bdc, tabs = _s5_tables(s5_lambda_re[l], s5_lambda_im[l], s5_log_dt[l], s5_b_re[l], s5_b_im[l],
                                    s5_c_re[l], s5_c_im[l], S5_TILE // 8)
        o_b = _s5_branch(proj, bdb, bdc, tabs, s5_d[l][None, :], s5_w_glu[l].astype(BF16), bsz, seq, S5_TILE)

        kv = _memory_kv(mem, mem_norm_g[l][None, :], w_kv_mem[l].astype(BF16))
        o_c = _xattn_branch(proj, kv, bsz, seq)

        out = _merge_output(o_a, o_b, o_c, proj, h, w_br_a[l].astype(BF16), w_br_b[l].astype(BF16),
                            w_br_c[l].astype(BF16), w_out[l].astype(BF16), final_g[None, :])
    return out.reshape(bsz, seq, d)
```

```python
import functools
import math

import jax
import jax.numpy as jnp
from jax import lax
from jax.experimental import pallas as pl
from jax.experimental.pallas import tpu as pltpu

F32 = jnp.float32
BF16 = jnp.bfloat16
EPS = 1e-6

D_MODEL = 2048
CHUNK = 64
N_HEADS = 8
HEAD_DIM = 128
GDN_WIDTH = N_HEADS * HEAD_DIM
CONV_WIDTH = 4
S5_WIDTH = 1024
S5_GROUP = 16
S5_GROUPS = 64
S5_STATE = 64
S5_SLABS = 4
S5_TILE = 256
XA_HEADS = 4
XA_HEAD_DIM = 256
XA_WIDTH = 1024
MEM_LEN = 256
VMEM_LIMIT = 56 * 1024 * 1024

COL_QKV = 0
COL_ZA = 3
COL_XB = 4
COL_ZB = 5
COL_QC = 6
COL_ZC = 7
COL_GATES = 8
N_MAIN = 14 * 1024


def _mm(a, b):
    return jnp.dot(a.astype(BF16), b.astype(BF16), preferred_element_type=F32)


def _mm_nt(a, b):
    return lax.dot_general(a.astype(BF16), b.astype(BF16), (((1,), (1,)), ((), ())),
                           preferred_element_type=F32)


def _mm_tn(a, b):
    return lax.dot_general(a.astype(BF16), b.astype(BF16), (((0,), (0,)), ((), ())),
                           preferred_element_type=F32)


def _split3(x):
    x0 = x.astype(BF16)
    r = x - x0.astype(F32)
    x1 = r.astype(BF16)
    r = r - x1.astype(F32)
    return x0, x1, r.astype(BF16)


def _sel_rhs(x, sel3):
    return jnp.dot(jnp.concatenate(_split3(x), axis=1), sel3, preferred_element_type=F32)


def _sel_lhs(sel3, x):
    return jnp.dot(sel3, jnp.concatenate(_split3(x), axis=0), preferred_element_type=F32)


def _sigmoid(x):
    return 1.0 / (1.0 + jnp.exp(-x))


def _silu(x):
    return x * _sigmoid(x)


def _softplus(x):
    return jnp.maximum(x, 0.0) + jnp.log1p(jnp.exp(-jnp.abs(x)))


def _gelu_tanh(x):
    return 0.5 * x * (1.0 + jnp.tanh(math.sqrt(2.0 / math.pi) * (x + 0.044715 * (x * x * x))))


def _repack_kernel(main_ref, tail_ref, o_ref, *, na, skip):
    j = pl.program_id(1)

    @pl.when(j < na)
    def _():
        o_ref[...] = main_ref[...].astype(o_ref.dtype)

    @pl.when(j >= na)
    def _():
        tn = o_ref.shape[1]
        cat = jnp.concatenate([main_ref[...], tail_ref[...]], axis=1)
        o_ref[...] = cat[:, skip:skip + tn].astype(o_ref.dtype)


def _repack_in_weights(wl, n_before, skip, tr=512, tn=1024):
    d, n = wl.shape
    n_out = n - skip
    lanes = 128
    return pl.pallas_call(
        functools.partial(_repack_kernel, na=n_before // tn, skip=skip),
        grid=(d // tr, n_out // tn),
        in_specs=[
            pl.BlockSpec((tr, tn), lambda i, j: (i, j)),
            pl.BlockSpec((tr, lanes), lambda i, j: (i, (j + 1) * (tn // lanes))),
        ],
        out_specs=pl.BlockSpec((tr, tn), lambda i, j: (i, j)),
        out_shape=jax.ShapeDtypeStruct((d, n_out), BF16),
        compiler_params=pltpu.CompilerParams(
            dimension_semantics=("parallel", "parallel"), vmem_limit_bytes=VMEM_LIMIT),
        name="repack_in_weights",
    )(wl, wl)


def _inproj_kernel(x_ref, g_ref, w_ref, ws_ref, wst_ref, o_ref, os_ref, ost_ref, u_scr):
    @pl.when(pl.program_id(1) == 0)
    def _():
        x = x_ref[...]
        ms = jnp.mean(x * x, axis=-1, keepdims=True)
        u = ((x * lax.rsqrt(ms + EPS)) * g_ref[...]).astype(BF16)
        u_scr[...] = u
        os_ref[...] = jnp.dot(u, ws_ref[...], preferred_element_type=F32)
        ost_ref[...] = lax.dot_general(wst_ref[...], u, (((1,), (1,)), ((), ())),
                                       preferred_element_type=F32)

    o_ref[...] = jnp.dot(u_scr[...], w_ref[...], preferred_element_type=F32)


def _in_projection(x2d, norm_g, w_main, w_small, w_small_t, tm=1024, tn=1024):
    t, d = x2d.shape
    n = w_main.shape[1]
    return pl.pallas_call(
        _inproj_kernel,
        grid=(t // tm, n // tn),
        in_specs=[
            pl.BlockSpec((tm, d), lambda i, j: (i, 0)),
            pl.BlockSpec((1, d), lambda i, j: (0, 0)),
            pl.BlockSpec((d, tn), lambda i, j: (0, j)),
            pl.BlockSpec((d, 128), lambda i, j: (0, 0)),
            pl.BlockSpec((16, d), lambda i, j: (0, 0)),
        ],
        out_specs=[
            pl.BlockSpec((tm, tn), lambda i, j: (i, j)),
            pl.BlockSpec((tm, 128), lambda i, j: (i, 0)),
            pl.BlockSpec((16, tm), lambda i, j: (0, i)),
        ],
        out_shape=[
            jax.ShapeDtypeStruct((t, n), F32),
            jax.ShapeDtypeStruct((t, 128), F32),
            jax.ShapeDtypeStruct((16, t), F32),
        ],
        scratch_shapes=[pltpu.VMEM((tm, d), BF16)],
        compiler_params=pltpu.CompilerParams(
            dimension_semantics=("parallel", "arbitrary"), vmem_limit_bytes=VMEM_LIMIT),
        name="in_projection",
    )(x2d, norm_g, w_main, w_small, w_small_t)


def _each(f, *lists):
    return [f(*args) for args in zip(*lists)]


def _unit_lower_inverse(lows, eye, blk16, blk32):
    n1 = _each(lambda l: jnp.where(blk16, -l, 0.0), lows)
    n2 = _each(_mm, n1, n1)
    n4 = _each(_mm, n2, n2)
    n3 = _each(_mm, n1, n2)
    n8 = _each(_mm, n4, n4)
    p = _each(lambda a, b, d: eye + a + b + d, n1, n2, n3)
    p = _each(lambda a, b: a + _mm(a, b), p, n4)
    p = _each(lambda a, b: a + _mm(a, b), p, n8)
    for mask in (jnp.logical_and(blk32, jnp.logical_not(blk16)), jnp.logical_not(blk32)):
        pe = _each(lambda a, l: _mm(a, jnp.where(mask, l, 0.0)), p, lows)
        p = _each(lambda a, b: a - _mm(b, a), p, pe)
    return p


def _gdn_kernel(qkv_ref, za_ref, sm_ref, smt_ref, convw_ref, expand_ref, tri_ref, triup_ref, alog_row_ref,
                dtb_row_ref, alog_col_ref, dtb_col_ref, ng_ref, o_ref, ext_scr, state_scr):
    c = CHUNK
    tt = qkv_ref.shape[0]
    nch = tt // c

    @pl.when(pl.program_id(1) == 0)
    def _():
        ext_scr[0:8, :] = jnp.zeros((8, 3 * GDN_WIDTH), F32)
        state_scr[...] = jnp.zeros_like(state_scr)

    raw = qkv_ref[...]
    ext_scr[8:8 + tt, :] = raw
    conv = convw_ref[3:4, :] * raw
    for j in range(CONV_WIDTH - 1):
        conv = conv + convw_ref[j:j + 1, :] * ext_scr[5 + j:5 + j + tt, :]
    ext_scr[0:8, :] = raw[tt - 8:tt, :]
    qkv = _silu(conv)

    row = lax.broadcasted_iota(jnp.int32, (c, c), 0)
    col = lax.broadcasted_iota(jnp.int32, (c, c), 1)
    incl = row >= col
    strict = row > col
    blk16 = (row >> 4) == (col >> 4)
    blk32 = (row >> 5) == (col >> 5)
    eye = jnp.where(row == col, 1.0, 0.0).astype(F32)
    sm = sm_ref[...]
    beta_c = _sigmoid(sm)
    g_c = -jnp.exp(alog_row_ref[...]) * _softplus(sm + dtb_row_ref[...])
    gc_c = _sel_lhs(tri_ref[...], g_c)
    gcl_c = jnp.concatenate([jnp.broadcast_to(gc_c[(ci + 1) * c - 1:(ci + 1) * c, :], (c, 128))
                             for ci in range(nch)], axis=0)
    beta_b = _sel_rhs(beta_c, expand_ref[0])
    gc_b = _sel_rhs(gc_c, expand_ref[1])
    egc_b = _sel_rhs(jnp.exp(gc_c), expand_ref[1])
    kd_b = _sel_rhs(jnp.exp(gcl_c - gc_c), expand_ref[1])
    gl_b = _sel_rhs(jnp.exp(gcl_c), expand_ref[1])
    gc_r = []
    for ci in range(nch):
        g_r = -jnp.exp(alog_col_ref[...]) * _softplus(smt_ref[ci] + dtb_col_ref[...])
        gc_r.append(_sel_rhs(g_r, triup_ref[...])[N_HEADS:, :])

    chains = [(ci, h) for ci in range(nch) for h in range(N_HEADS)]

    def part(arr, ci, h, base=0):
        return arr[ci * c:(ci + 1) * c, base + h * HEAD_DIM:base + (h + 1) * HEAD_DIM]

    def l2n(a):
        return a * lax.rsqrt(jnp.sum(a * a, axis=-1, keepdims=True) + EPS)

    q = [l2n(part(qkv, ci, h)) * (HEAD_DIM ** -0.5) for ci, h in chains]
    k = [l2n(part(qkv, ci, h, GDN_WIDTH)) for ci, h in chains]
    v = [part(qkv, ci, h, 2 * GDN_WIDTH) for ci, h in chains]
    beta = [part(beta_b, ci, h) for ci, h in chains]
    gcb = [part(gc_b, ci, h) for ci, h in chains]
    egc = [part(egc_b, ci, h) for ci, h in chains]
    kb = _each(lambda a, b: a * b, k, beta)
    aq = _each(lambda a, b, d: _mm_nt(jnp.concatenate([a, b], axis=0), d), kb, q, k)
    decay = [jnp.exp(jnp.where(incl, g[:, :c] - gc_r[ci][h:h + 1, :], -jnp.inf))
             for g, (ci, h) in zip(gcb, chains)]
    low = _each(lambda a, d: jnp.where(strict, a[:c] * d, 0.0), aq, decay)
    tinv = _unit_lower_inverse(low, eye, blk16, blk32)
    sol = _each(lambda t, a, b, d, e: _mm(t, jnp.concatenate([a * b, d * e], axis=1)),
                tinv, v, beta, kb, egc)
    wq = _each(lambda s, a, e: jnp.concatenate([s[:, HEAD_DIM:], a * e], axis=0), sol, q, egc)
    qkd = _each(lambda a, d: a[c:] * d, aq, decay)
    k_dec = [a * part(kd_b, ci, h) for a, (ci, h) in zip(k, chains)]
    g_last = [part(gl_b, ci, h)[0:1, :] for ci, h in chains]

    states = [state_scr[h] for h in range(N_HEADS)]
    outs = []
    for ci in range(nch):
        idx = [ci * N_HEADS + h for h in range(N_HEADS)]
        ws = [_mm(wq[i], s) for i, s in zip(idx, states)]
        v_new = [sol[i][:, :HEAD_DIM] - w[:c] for i, w in zip(idx, ws)]
        o = [w[c:] + _mm(qkd[i], vn) for i, w, vn in zip(idx, ws, v_new)]
        states = [s * g_last[i] + _mm_tn(k_dec[i], vn) for i, s, vn in zip(idx, states, v_new)]
        o = [a * lax.rsqrt(jnp.mean(a * a, axis=-1, keepdims=True) + EPS) for a in o]
        outs.append(jnp.concatenate(o, axis=1))
    o_tile = jnp.concatenate(outs, axis=0) if nch > 1 else outs[0]
    o_ref[...] = ((o_tile * ng_ref[...]) * _silu(za_ref[...])).astype(o_ref.dtype)
    for h in range(N_HEADS):
        state_scr[h] = states[h]


def _gdn_branch(proj, small, small_t3, conv_w, a_log, dt_bias, norm_g, bsz, seq, tt=4 * CHUNK):
    nc = seq // tt
    nch = tt // CHUNK
    t = bsz * seq
    full = lambda shape: pl.BlockSpec(shape, lambda b, c: (0,) * len(shape))
    lane = jnp.arange(128)[:, None]
    head_of_col = jnp.arange(GDN_WIDTH)[None, :] // HEAD_DIM
    expand = jnp.stack([jnp.tile(lane == head_of_col, (3, 1)),
                        jnp.tile(lane == head_of_col + N_HEADS, (3, 1))]).astype(BF16)
    it = jnp.arange(tt)
    tri = jnp.logical_and(it[:, None] >= it[None, :], it[:, None] // CHUNK == it[None, :] // CHUNK)
    tri3 = jnp.tile(tri, (1, 3)).astype(BF16)
    ic = jnp.arange(CHUNK)
    triup3 = jnp.tile(ic[:, None] <= ic[None, :], (3, 1)).astype(BF16)
    a_log = a_log.astype(F32)
    dt_bias = dt_bias.astype(F32)
    pad_lanes = lambda a: jnp.pad(a, (N_HEADS, 128 - 2 * N_HEADS))[None, :]
    alog_row, dtb_row = pad_lanes(a_log), pad_lanes(dt_bias)
    alog_col, dtb_col = jnp.pad(a_log, (N_HEADS, 0))[:, None], jnp.pad(dt_bias, (N_HEADS, 0))[:, None]
    norm_g = jnp.tile(norm_g.astype(F32), N_HEADS)[None, :]
    return pl.pallas_call(
        _gdn_kernel,
        grid=(bsz, nc),
        in_specs=[
            pl.BlockSpec((tt, 3 * GDN_WIDTH), lambda b, c: (b * nc + c, COL_QKV)),
            pl.BlockSpec((tt, GDN_WIDTH), lambda b, c: (b * nc + c, COL_ZA)),
            pl.BlockSpec((tt, 128), lambda b, c: (b * nc + c, 0)),
            pl.BlockSpec((nch, 16, CHUNK), lambda b, c: (b * nc + c, 0, 0)),
            full((CONV_WIDTH, 3 * GDN_WIDTH)),
            full((2, 3 * 128, GDN_WIDTH)),
            full((tt, 3 * tt)),
            full((3 * CHUNK, CHUNK)),
            full((1, 128)),
            full((1, 128)),
            full((2 * N_HEADS, 1)),
            full((2 * N_HEADS, 1)),
            full((1, GDN_WIDTH)),
        ],
        out_specs=pl.BlockSpec((tt, GDN_WIDTH), lambda b, c: (b * nc + c, 0)),
        out_shape=jax.ShapeDtypeStruct((t, GDN_WIDTH), BF16),
        scratch_shapes=[
            pltpu.VMEM((tt + 8, 3 * GDN_WIDTH), F32),
            pltpu.VMEM((N_HEADS, HEAD_DIM, HEAD_DIM), F32),
        ],
        compiler_params=pltpu.CompilerParams(
            dimension_semantics=("parallel", "arbitrary"), vmem_limit_bytes=VMEM_LIMIT),
        name="gated_deltanet",
    )(proj, proj, small, small_t3, conv_w, expand, tri3, triup3, alog_row, dtb_row, alog_col, dtb_col, norm_g)


def _s5_kernel(xb_ref, zb_ref, perm_ref, permt_ref, bdb_ref, bdc_ref, tab_ref, d_ref, wglu_ref, o_ref,
               h_scr, carry_scr):
    tt = xb_ref.shape[0]
    nj = tt // 8
    ns = S5_GROUPS * S5_STATE // S5_SLABS

    @pl.when(pl.program_id(1) == 0)
    def _():
        carry_scr[...] = jnp.zeros_like(carry_scr)

    def cmul(ar, ai, br, bi):
        return ar * br - ai * bi, ar * bi + ai * br

    xp = _sel_lhs(perm_ref[...], xb_ref[...])
    xpb = xp.astype(BF16)
    not_first = lax.broadcasted_iota(jnp.int32, (8, ns), 0) >= 1

    ys = []
    for s in range(S5_SLABS):
        h_scr[s] = jnp.dot(xpb[:, s * 256:(s + 1) * 256], bdb_ref[s], preferred_element_type=F32)
        ar, ai = tab_ref[s, 0, 0], tab_ref[s, 0, 1]
        er = h_scr[s, 0:8, :ns]
        ei = h_scr[s, 0:8, ns:]
        for j in range(1, nj):
            mr, mi = cmul(ar, ai, er, ei)
            er = mr + h_scr[s, j * 8:(j + 1) * 8, :ns]
            ei = mi + h_scr[s, j * 8:(j + 1) * 8, ns:]
        for lvl, dist in enumerate((1, 2, 4)):
            mr, mi = cmul(tab_ref[s, 1 + lvl, 0], tab_ref[s, 1 + lvl, 1],
                          pltpu.roll(er, dist, 0), pltpu.roll(ei, dist, 0))
            er, ei = er + mr, ei + mi
        cr, ci = carry_scr[s, 0], carry_scr[s, 1]
        mr, mi = cmul(tab_ref[s, 4, 0], tab_ref[s, 4, 1], cr, ci)
        gr = jnp.where(not_first, pltpu.roll(er, 1, 0), 0.0) + mr
        gi = jnp.where(not_first, pltpu.roll(ei, 1, 0), 0.0) + mi
        mr, mi = cmul(tab_ref[s, 5, 0], tab_ref[s, 5, 1], cr, ci)
        carry_scr[s, 0] = jnp.broadcast_to(er[7:8, :], (8, ns)) + mr
        carry_scr[s, 1] = jnp.broadcast_to(ei[7:8, :], (8, ns)) + mi
        for j in range(nj):
            mr, mi = cmul(ar, ai, gr, gi)
            gr = mr + h_scr[s, j * 8:(j + 1) * 8, :ns]
            gi = mi + h_scr[s, j * 8:(j + 1) * 8, ns:]
            h_scr[s, j * 8:(j + 1) * 8, :ns] = gr
            h_scr[s, j * 8:(j + 1) * 8, ns:] = gi
        ys.append(jnp.dot(h_scr[s, :, :ns].astype(BF16), bdc_ref[s, :ns, :], preferred_element_type=F32)
                  + jnp.dot(h_scr[s, :, ns:].astype(BF16), bdc_ref[s, ns:, :], preferred_element_type=F32))

    y = _gelu_tanh(jnp.concatenate(ys, axis=1) + d_ref[...] * xp)
    glu = jnp.dot(y.astype(BF16), wglu_ref[...], preferred_element_type=F32)
    o_b = (glu[:, :S5_WIDTH] * _sigmoid(glu[:, S5_WIDTH:])).astype(BF16)
    o_b = jnp.dot(permt_ref[...], o_b, preferred_element_type=F32)
    o_ref[...] = (o_b * _silu(zb_ref[...])).astype(o_ref.dtype)


def _s5_branch(proj, bdb, bdc, tabs, d_row, w_glu, bsz, seq, tt=256):
    nt = seq // tt
    ns = S5_GROUPS * S5_STATE // S5_SLABS
    rho = jnp.arange(tt)
    perm = ((rho % 8) * (tt // 8) + rho // 8)[:, None] == jnp.arange(tt)[None, :]
    full = lambda shape: pl.BlockSpec(shape, lambda b, i: (0,) * len(shape))
    return pl.pallas_call(
        _s5_kernel,
        grid=(bsz, nt),
        in_specs=[
            pl.BlockSpec((tt, S5_WIDTH), lambda b, i: (b * nt + i, COL_XB)),
            pl.BlockSpec((tt, S5_WIDTH), lambda b, i: (b * nt + i, COL_ZB)),
            full((tt, 3 * tt)),
            full((tt, tt)),
            full((S5_SLABS, 256, 2 * ns)),
            full((S5_SLABS, 2 * ns, 256)),
            full((S5_SLABS, 6, 2, 8, ns)),
            full((1, S5_WIDTH)),
            full((S5_WIDTH, 2 * S5_WIDTH)),
        ],
        out_specs=pl.BlockSpec((tt, S5_WIDTH), lambda b, i: (b * nt + i, 0)),
        out_shape=jax.ShapeDtypeStruct((bsz * seq, S5_WIDTH), BF16),
        scratch_shapes=[
            pltpu.VMEM((S5_SLABS, tt, 2 * ns), F32),
            pltpu.VMEM((S5_SLABS, 2, 8, ns), F32),
        ],
        compiler_params=pltpu.CompilerParams(
            dimension_semantics=("parallel", "arbitrary"), vmem_limit_bytes=VMEM_LIMIT),
        name="s5_layer",
    )(proj, proj, jnp.tile(perm, (1, 3)).astype(BF16), perm.T.astype(BF16), bdb, bdc, tabs, d_row, w_glu)


def _s5_tables(lam_re, lam_im, log_dt, b_re, b_im, c_re, c_im, seg):
    g, p, gi = S5_GROUPS, S5_STATE, S5_GROUP
    gs = g // S5_SLABS
    lr, li = lam_re.astype(F32), lam_im.astype(F32)
    dt = jnp.exp(log_dt.astype(F32))[:, None]
    mag = jnp.exp(lr * dt)
    ab_re, ab_im = mag * jnp.cos(li * dt), mag * jnp.sin(li * dt)
    den = lr * lr + li * li
    nr, ni = ab_re - 1.0, ab_im
    coef_re = (nr * lr + ni * li) / den
    coef_im = (ni * lr - nr * li) / den
    br, bi = b_re.astype(F32), b_im.astype(F32)
    bb_re = coef_re[..., None] * br - coef_im[..., None] * bi
    bb_im = coef_re[..., None] * bi + coef_im[..., None] * br

    eye = jnp.eye(gs, dtype=F32)

    def blockdiag_in(bb):
        bb = bb.reshape(S5_SLABS, gs, p, gi)
        return jnp.einsum('sgpi,gh->sgihp', bb, eye).reshape(S5_SLABS, gs * gi, gs * p)

    def blockdiag_out(cc):
        cc = cc.reshape(S5_SLABS, gs, gi, p)
        return jnp.einsum('sgip,gh->sgphi', cc, eye).reshape(S5_SLABS, gs * p, gs * gi)

    bdb = jnp.concatenate([blockdiag_in(bb_re), blockdiag_in(bb_im)], axis=2).astype(BF16)
    bdc = jnp.concatenate([blockdiag_out(c_re.astype(F32)), -blockdiag_out(c_im.astype(F32))],
                          axis=1).astype(BF16)

    def cmul(a, b):
        return a[0] * b[0] - a[1] * b[1], a[0] * b[1] + a[1] * b[0]

    p1 = (ab_re.reshape(S5_SLABS, gs * p), ab_im.reshape(S5_SLABS, gs * p))
    pseg = p1
    for _ in range(int(math.log2(seg))):
        pseg = cmul(pseg, pseg)
    pseg2 = cmul(pseg, pseg)
    pseg4 = cmul(pseg2, pseg2)
    ptile = cmul(pseg4, pseg4)
    rows = jnp.arange(8)[None, None, :, None]

    def bcast(pw):
        return jnp.broadcast_to(jnp.stack([pw[0], pw[1]], axis=1)[:, :, None, :], (S5_SLABS, 2, 8, gs * p))

    def masked(pw, dist):
        return jnp.where(rows >= dist, bcast(pw), 0.0)

    pows = [(jnp.ones_like(p1[0]), jnp.zeros_like(p1[0]))]
    for _ in range(7):
        pows.append(cmul(pows[-1], pseg))
    ramp = jnp.stack([jnp.stack([pw[0] for pw in pows], axis=1),
                      jnp.stack([pw[1] for pw in pows], axis=1)], axis=1)
    tabs = jnp.stack([bcast(p1), masked(pseg, 1), masked(pseg2, 2), masked(pseg4, 4), ramp, bcast(ptile)],
                     axis=1)
    return bdb, bdc, tabs


def _kv_kernel(mem_ref, g_ref, w_ref, o_ref):
    m = mem_ref[0]
    ms = jnp.mean(m * m, axis=-1, keepdims=True)
    mn = ((m * lax.rsqrt(ms + EPS)) * g_ref[...]).astype(BF16)
    o_ref[0] = jnp.dot(mn, w_ref[...], preferred_element_type=F32).astype(o_ref.dtype)


def _memory_kv(mem, mem_norm_g, w_kv):
    bsz = mem.shape[0]
    return pl.pallas_call(
        _kv_kernel,
        grid=(bsz,),
        in_specs=[
            pl.BlockSpec((1, MEM_LEN, D_MODEL), lambda b: (b, 0, 0)),
            pl.BlockSpec((1, D_MODEL), lambda b: (0, 0)),
            pl.BlockSpec((D_MODEL, 2 * XA_WIDTH), lambda b: (0, 0)),
        ],
        out_specs=pl.BlockSpec((1, MEM_LEN, 2 * XA_WIDTH), lambda b: (b, 0, 0)),
        out_shape=jax.ShapeDtypeStruct((bsz, MEM_LEN, 2 * XA_WIDTH), BF16),
        compiler_params=pltpu.CompilerParams(
            dimension_semantics=("parallel",), vmem_limit_bytes=VMEM_LIMIT),
        name="memory_kv",
    )(mem, mem_norm_g, w_kv)


def _xattn_kernel(q_ref, zc_ref, kv_ref, o_ref):
    scale = XA_HEAD_DIM ** -0.5
    for h in range(XA_HEADS):
        lo = h * XA_HEAD_DIM
        q = q_ref[:, lo:lo + XA_HEAD_DIM].astype(BF16)
        k = kv_ref[0, :, lo:lo + XA_HEAD_DIM]
        v = kv_ref[0, :, XA_WIDTH + lo:XA_WIDTH + lo + XA_HEAD_DIM]
        s = lax.dot_general(q, k, (((1,), (1,)), ((), ())), preferred_element_type=F32) * scale
        e = jnp.exp(s - jnp.max(s, axis=-1, keepdims=True))
        p = e / jnp.sum(e, axis=-1, keepdims=True)
        o = jnp.dot(p.astype(BF16), v, preferred_element_type=F32)
        o_ref[:, lo:lo + XA_HEAD_DIM] = (o * _silu(zc_ref[:, lo:lo + XA_HEAD_DIM])).astype(o_ref.dtype)


def _xattn_branch(proj, kv, bsz, seq, tt=512):
    nt = seq // tt
    return pl.pallas_call(
        _xattn_kernel,
        grid=(bsz, nt),
        in_specs=[
            pl.BlockSpec((tt, XA_WIDTH), lambda b, i: (b * nt + i, COL_QC)),
            pl.BlockSpec((tt, XA_WIDTH), lambda b, i: (b * nt + i, COL_ZC)),
            pl.BlockSpec((1, MEM_LEN, 2 * XA_WIDTH), lambda b, i: (b, 0, 0)),
        ],
        out_specs=pl.BlockSpec((tt, XA_WIDTH), lambda b, i: (b * nt + i, 0)),
        out_shape=jax.ShapeDtypeStruct((bsz * seq, XA_WIDTH), BF16),
        compiler_params=pltpu.CompilerParams(
            dimension_semantics=("parallel", "parallel"), vmem_limit_bytes=VMEM_LIMIT),
        name="memory_xattn",
    )(proj, proj, kv)


def _merge_kernel(a_ref, b_ref, c_ref, g0_ref, g1_ref, g2_ref, x_ref, wa_ref, wb_ref, wc_ref,
                  wo_ref, fg_ref, o_ref):
    merged = _sigmoid(g0_ref[...]) * jnp.dot(a_ref[...], wa_ref[...], preferred_element_type=F32)
    merged = merged + _sigmoid(g1_ref[...]) * jnp.dot(b_ref[...], wb_ref[...], preferred_element_type=F32)
    merged = merged + _sigmoid(g2_ref[...]) * jnp.dot(c_ref[...], wc_ref[...], preferred_element_type=F32)
    h = x_ref[...] + jnp.dot(merged.astype(BF16), wo_ref[...], preferred_element_type=F32)
    ms = jnp.mean(h * h, axis=-1, keepdims=True)
    o_ref[...] = (h * lax.rsqrt(ms + EPS)) * fg_ref[...]


def _merge_output(a, b, c, proj, x2d, wa, wb, wc, wo, final_g, tm=256):
    t = x2d.shape[0]
    gb = COL_GATES // 2
    const = lambda shape: pl.BlockSpec(shape, lambda i: (0,) * len(shape), pipeline_mode=pl.Buffered(1))
    return pl.pallas_call(
        _merge_kernel,
        grid=(t // tm,),
        in_specs=[
            pl.BlockSpec((tm, GDN_WIDTH), lambda i: (i, 0)),
            pl.BlockSpec((tm, S5_WIDTH), lambda i: (i, 0)),
            pl.BlockSpec((tm, XA_WIDTH), lambda i: (i, 0)),
            pl.BlockSpec((tm, D_MODEL), lambda i: (i, gb)),
            pl.BlockSpec((tm, D_MODEL), lambda i: (i, gb + 1)),
            pl.BlockSpec((tm, D_MODEL), lambda i: (i, gb + 2)),
            pl.BlockSpec((tm, D_MODEL), lambda i: (i, 0)),
            const((GDN_WIDTH, D_MODEL)),
            const((S5_WIDTH, D_MODEL)),
            const((XA_WIDTH, D_MODEL)),
            const((D_MODEL, D_MODEL)),
            const((1, D_MODEL)),
        ],
        out_specs=pl.BlockSpec((tm, D_MODEL), lambda i: (i, 0)),
        out_shape=jax.ShapeDtypeStruct((t, D_MODEL), F32),
        compiler_params=pltpu.CompilerParams(
            dimension_semantics=("parallel",), vmem_limit_bytes=VMEM_LIMIT),
        name="merge_output",
    )(a, b, c, proj, proj, proj, x2d, wa, wb, wc, wo, final_g)


def kernel(x, mem, norm_g, w_in, conv_w, gdn_a_log, gdn_dt_bias, gdn_norm_g, s5_lambda_re, s5_lambda_im,
           s5_log_dt, s5_b_re, s5_b_im, s5_c_re, s5_c_im, s5_d, s5_w_glu, mem_norm_g, w_kv_mem, w_br_a,
           w_br_b, w_br_c, w_out, final_g):
    bsz, seq, d = x.shape
    t = bsz * seq
    h = x.reshape(t, d)
    assert norm_g.shape[0] == 1, "the merge kernel fuses the final RMSNorm: single layer only"
    for l in range(1):
        wl = w_in[l]
        n_a = 4 * GDN_WIDTH
        w_main = _repack_in_weights(wl, n_a, 2 * N_HEADS)
        w_small = wl[:, n_a:n_a + 2 * N_HEADS]
        w_small_t = w_small.T.astype(BF16)
        w_small = jnp.pad(w_small, ((0, 0), (0, 128 - 2 * N_HEADS))).astype(BF16)

        proj, small, small_t = _in_projection(h, norm_g[l][None, :], w_main, w_small, w_small_t)

        small_t3 = small_t.reshape(2 * N_HEADS, t // CHUNK, CHUNK).transpose(1, 0, 2)
        o_a = _gdn_branch(proj, small, small_t3, conv_w[l], gdn_a_log[l], gdn_dt_bias[l], gdn_norm_g[l],
                          bsz, seq)

        bdb, bdc, tabs = _s5_tables(s5_lambda_re[l], s5_lambda_im[l], s5_log_dt[l], s5_b_re[l], s5_b_im[l],
                                    s5_c_re[l], s5_c_im[l], S5_TILE // 8)
        o_b = _s5_branch(proj, bdb, bdc, tabs, s5_d[l][None, :], s5_w_glu[l].astype(BF16), bsz, seq, S5_TILE)

        kv = _memory_kv(mem, mem_norm_g[l][None, :], w_kv_mem[l].astype(BF16))
        o_c = _xattn_branch(proj, kv, bsz, seq)

        out = _merge_output(o_a, o_b, o_c, proj, h, w_br_a[l].astype(BF16), w_br_b[l].astype(BF16),
                            w_br_c[l].astype(BF16), w_out[l].astype(BF16), final_g[None, :])
    return out.reshape(bsz, seq, d)
```

```python
import functools
import math

import jax
import jax.numpy as jnp
from jax import lax
from jax.experimental import pallas as pl
from jax.experimental.pallas import tpu as pltpu

F32 = jnp.float32
BF16 = jnp.bfloat16
EPS = 1e-6

D_MODEL = 2048
CHUNK = 64
N_HEADS = 8
HEAD_DIM = 128
GDN_WIDTH = N_HEADS * HEAD_DIM
CONV_WIDTH = 4
S5_WIDTH = 1024
S5_GROUP = 16
S5_GROUPS = 64
S5_STATE = 64
S5_SLABS = 4
S5_TILE = 256
XA_HEADS = 4
XA_HEAD_DIM = 256
XA_WIDTH = 1024
MEM_LEN = 256
VMEM_LIMIT = 56 * 1024 * 1024

COL_QKV = 0
COL_ZA = 3
COL_XB = 4
COL_ZB = 5
COL_QC = 6
COL_ZC = 7
COL_GATES = 8
N_MAIN = 14 * 1024


def _mm(a, b):
    return jnp.dot(a.astype(BF16), b.astype(BF16), preferred_element_type=F32)


def _mm_nt(a, b):
    return lax.dot_general(a.astype(BF16), b.astype(BF16), (((1,), (1,)), ((), ())),
                           preferred_element_type=F32)


def _mm_tn(a, b):
    return lax.dot_general(a.astype(BF16), b.astype(BF16), (((0,), (0,)), ((), ())),
                           preferred_element_type=F32)


def _split3(x):
    x0 = x.astype(BF16)
    r = x - x0.astype(F32)
    x1 = r.astype(BF16)
    r = r - x1.astype(F32)
    return x0, x1, r.astype(BF16)


def _sel_rhs(x, sel3):
    return jnp.dot(jnp.concatenate(_split3(x), axis=1), sel3, preferred_element_type=F32)


def _sel_lhs(sel3, x):
    return jnp.dot(sel3, jnp.concatenate(_split3(x), axis=0), preferred_element_type=F32)


def _sigmoid(x):
    return 1.0 / (1.0 + jnp.exp(-x))


def _silu(x):
    return x * _sigmoid(x)


def _softplus(x):
    return jnp.maximum(x, 0.0) + jnp.log1p(jnp.exp(-jnp.abs(x)))


def _gelu_tanh(x):
    return 0.5 * x * (1.0 + jnp.tanh(math.sqrt(2.0 / math.pi) * (x + 0.044715 * (x * x * x))))


def _inproj_kernel(x_ref, g_ref, wt_ref, wst_ref, o_ref, os_ref, ost_ref, u_scr):
    nt_dims = (((1,), (1,)), ((), ()))

    @pl.when(pl.program_id(1) == 0)
    def _():
        x = x_ref[...]
        ms = jnp.mean(x * x, axis=-1, keepdims=True)
        u = ((x * lax.rsqrt(ms + EPS)) * g_ref[...]).astype(BF16)
        u_scr[...] = u
        ws = wst_ref[...].astype(BF16)
        ost_ref[...] = lax.dot_general(ws, u, nt_dims, preferred_element_type=F32)
        ws = jnp.concatenate([ws, jnp.zeros((os_ref.shape[1] - ws.shape[0], ws.shape[1]), BF16)], axis=0)
        os_ref[...] = lax.dot_general(u, ws, nt_dims, preferred_element_type=F32)

    o_ref[...] = lax.dot_general(u_scr[...], wt_ref[...].astype(BF16), nt_dims, preferred_element_type=F32)


def _in_projection(x2d, norm_g, w_t, n_before, skip, tm=1024, tn=1024):
    t, d = x2d.shape
    n = w_t.shape[0] - skip
    na = n_before // tn
    return pl.pallas_call(
        _inproj_kernel,
        grid=(t // tm, n // tn),
        in_specs=[
            pl.BlockSpec((tm, d), lambda i, j: (i, 0)),
            pl.BlockSpec((1, d), lambda i, j: (0, 0)),
            pl.BlockSpec((pl.Element(tn), pl.Element(d)),
                         lambda i, j: ((j * (tn // 8) + jnp.where(j >= na, skip // 8, 0)) * 8, 0)),
            pl.BlockSpec((pl.Element(skip), pl.Element(d)), lambda i, j: (n_before, 0)),
        ],
        out_specs=[
            pl.BlockSpec((tm, tn), lambda i, j: (i, j)),
            pl.BlockSpec((tm, 128), lambda i, j: (i, 0)),
            pl.BlockSpec((skip, tm), lambda i, j: (0, i)),
        ],
        out_shape=[
            jax.ShapeDtypeStruct((t, n), F32),
            jax.ShapeDtypeStruct((t, 128), F32),
            jax.ShapeDtypeStruct((skip, t), F32),
        ],
        scratch_shapes=[pltpu.VMEM((tm, d), BF16)],
        compiler_params=pltpu.CompilerParams(
            dimension_semantics=("parallel", "arbitrary"), vmem_limit_bytes=VMEM_LIMIT),
        name="in_projection",
    )(x2d, norm_g, w_t, w_t)


def _each(f, *lists):
    return [f(*args) for args in zip(*lists)]


def _unit_lower_inverse(lows, eye, blk16, blk32):
    n1 = _each(lambda l: jnp.where(blk16, -l, 0.0), lows)
    n2 = _each(_mm, n1, n1)
    n4 = _each(_mm, n2, n2)
    n3 = _each(_mm, n1, n2)
    n8 = _each(_mm, n4, n4)
    p = _each(lambda a, b, d: eye + a + b + d, n1, n2, n3)
    p = _each(lambda a, b: a + _mm(a, b), p, n4)
    p = _each(lambda a, b: a + _mm(a, b), p, n8)
    for mask in (jnp.logical_and(blk32, jnp.logical_not(blk16)), jnp.logical_not(blk32)):
        pe = _each(lambda a, l: _mm(a, jnp.where(mask, l, 0.0)), p, lows)
        p = _each(lambda a, b: a - _mm(b, a), p, pe)
    return p


def _gdn_kernel(qkv_ref, za_ref, sm_ref, smt_ref, convw_ref, expand_ref, tri_ref, triup_ref, alog_row_ref,
                dtb_row_ref, alog_col_ref, dtb_col_ref, ng_ref, o_ref, ext_scr, state_scr):
    c = CHUNK
    tt = qkv_ref.shape[0]
    nch = tt // c

    @pl.when(pl.program_id(1) == 0)
    def _():
        ext_scr[0:8, :] = jnp.zeros((8, 3 * GDN_WIDTH), F32)
        state_scr[...] = jnp.zeros_like(state_scr)

    raw = qkv_ref[...]
    ext_scr[8:8 + tt, :] = raw
    conv = convw_ref[3:4, :] * raw
    for j in range(CONV_WIDTH - 1):
        conv = conv + convw_ref[j:j + 1, :] * ext_scr[5 + j:5 + j + tt, :]
    ext_scr[0:8, :] = raw[tt - 8:tt, :]
    qkv = _silu(conv)

    row = lax.broadcasted_iota(jnp.int32, (c, c), 0)
    col = lax.broadcasted_iota(jnp.int32, (c, c), 1)
    incl = row >= col
    strict = row > col
    blk16 = (row >> 4) == (col >> 4)
    blk32 = (row >> 5) == (col >> 5)
    eye = jnp.where(row == col, 1.0, 0.0).astype(F32)
    sm = sm_ref[...]
    beta_c = _sigmoid(sm)
    g_c = -jnp.exp(alog_row_ref[...]) * _softplus(sm + dtb_row_ref[...])
    gc_c = _sel_lhs(tri_ref[...], g_c)
    gcl_c = jnp.concatenate([jnp.broadcast_to(gc_c[(ci + 1) * c - 1:(ci + 1) * c, :], (c, 128))
                             for ci in range(nch)], axis=0)
    beta_b = _sel_rhs(beta_c, expand_ref[0])
    gc_b = _sel_rhs(gc_c, expand_ref[1])
    egc_b = _sel_rhs(jnp.exp(gc_c), expand_ref[1])
    kd_b = _sel_rhs(jnp.exp(gcl_c - gc_c), expand_ref[1])
    gl_b = _sel_rhs(jnp.exp(gcl_c), expand_ref[1])
    gc_r = []
    for ci in range(nch):
        g_r = -jnp.exp(alog_col_ref[...]) * _softplus(smt_ref[ci] + dtb_col_ref[...])
        gc_r.append(_sel_rhs(g_r, triup_ref[...])[N_HEADS:, :])

    chains = [(ci, h) for ci in range(nch) for h in range(N_HEADS)]

    def part(arr, ci, h, base=0):
        return arr[ci * c:(ci + 1) * c, base + h * HEAD_DIM:base + (h + 1) * HEAD_DIM]

    def l2n(a):
        return a * lax.rsqrt(jnp.sum(a * a, axis=-1, keepdims=True) + EPS)

    q = [l2n(part(qkv, ci, h)) * (HEAD_DIM ** -0.5) for ci, h in chains]
    k = [l2n(part(qkv, ci, h, GDN_WIDTH)) for ci, h in chains]
    v = [part(qkv, ci, h, 2 * GDN_WIDTH) for ci, h in chains]
    beta = [part(beta_b, ci, h) for ci, h in chains]
    gcb = [part(gc_b, ci, h) for ci, h in chains]
    egc = [part(egc_b, ci, h) for ci, h in chains]
    kb = _each(lambda a, b: a * b, k, beta)
    aq = _each(lambda a, b, d: _mm_nt(jnp.concatenate([a, b], axis=0), d), kb, q, k)
    decay = [jnp.exp(jnp.where(incl, g[:, :c] - gc_r[ci][h:h + 1, :], -jnp.inf))
             for g, (ci, h) in zip(gcb, chains)]
    low = _each(lambda a, d: jnp.where(strict, a[:c] * d, 0.0), aq, decay)
    tinv = _unit_lower_inverse(low, eye, blk16, blk32)
    sol = _each(lambda t, a, b, d, e: _mm(t, jnp.concatenate([a * b, d * e], axis=1)),
                tinv, v, beta, kb, egc)
    wq = _each(lambda s, a, e: jnp.concatenate([s[:, HEAD_DIM:], a * e], axis=0), sol, q, egc)
    qkd = _each(lambda a, d: a[c:] * d, aq, decay)
    k_dec = [a * part(kd_b, ci, h) for a, (ci, h) in zip(k, chains)]
    g_last = [part(gl_b, ci, h)[0:1, :] for ci, h in chains]

    states = [state_scr[h] for h in range(N_HEADS)]
    outs = []
    for ci in range(nch):
        idx = [ci * N_HEADS + h for h in range(N_HEADS)]
        ws = [_mm(wq[i], s) for i, s in zip(idx, states)]
        v_new = [sol[i][:, :HEAD_DIM] - w[:c] for i, w in zip(idx, ws)]
        o = [w[c:] + _mm(qkd[i], vn) for i, w, vn in zip(idx, ws, v_new)]
        states = [s * g_last[i] + _mm_tn(k_dec[i], vn) for i, s, vn in zip(idx, states, v_new)]
        o = [a * lax.rsqrt(jnp.mean(a * a, axis=-1, keepdims=True) + EPS) for a in o]
        outs.append(jnp.concatenate(o, axis=1))
    o_tile = jnp.concatenate(outs, axis=0) if nch > 1 else outs[0]
    o_ref[...] = ((o_tile * ng_ref[...]) * _silu(za_ref[...])).astype(o_ref.dtype)
    for h in range(N_HEADS):
        state_scr[h] = states[h]


def _gdn_branch(proj, small, small_t3, conv_w, a_log, dt_bias, norm_g, bsz, seq, tt=4 * CHUNK):
    nc = seq // tt
    nch = tt // CHUNK
    t = bsz * seq
    full = lambda shape: pl.BlockSpec(shape, lambda b, c: (0,) * len(shape))
    lane = jnp.arange(128)[:, None]
    head_of_col = jnp.arange(GDN_WIDTH)[None, :] // HEAD_DIM
    expand = jnp.stack([jnp.tile(lane == head_of_col, (3, 1)),
                        jnp.tile(lane == head_of_col + N_HEADS, (3, 1))]).astype(BF16)
    it = jnp.arange(tt)
    tri = jnp.logical_and(it[:, None] >= it[None, :], it[:, None] // CHUNK == it[None, :] // CHUNK)
    tri3 = jnp.tile(tri, (1, 3)).astype(BF16)
    ic = jnp.arange(CHUNK)
    triup3 = jnp.tile(ic[:, None] <= ic[None, :], (3, 1)).astype(BF16)
    a_log = a_log.astype(F32)
    dt_bias = dt_bias.astype(F32)
    pad_lanes = lambda a: jnp.pad(a, (N_HEADS, 128 - 2 * N_HEADS))[None, :]
    alog_row, dtb_row = pad_lanes(a_log), pad_lanes(dt_bias)
    alog_col, dtb_col = jnp.pad(a_log, (N_HEADS, 0))[:, None], jnp.pad(dt_bias, (N_HEADS, 0))[:, None]
    norm_g = jnp.tile(norm_g.astype(F32), N_HEADS)[None, :]
    return pl.pallas_call(
        _gdn_kernel,
        grid=(bsz, nc),
        in_specs=[
            pl.BlockSpec((tt, 3 * GDN_WIDTH), lambda b, c: (b * nc + c, COL_QKV)),
            pl.BlockSpec((tt, GDN_WIDTH), lambda b, c: (b * nc + c, COL_ZA)),
            pl.BlockSpec((tt, 128), lambda b, c: (b * nc + c, 0)),
            pl.BlockSpec((nch, 16, CHUNK), lambda b, c: (b * nc + c, 0, 0)),
            full((CONV_WIDTH, 3 * GDN_WIDTH)),
            full((2, 3 * 128, GDN_WIDTH)),
            full((tt, 3 * tt)),
            full((3 * CHUNK, CHUNK)),
            full((1, 128)),
            full((1, 128)),
            full((2 * N_HEADS, 1)),
            full((2 * N_HEADS, 1)),
            full((1, GDN_WIDTH)),
        ],
        out_specs=pl.BlockSpec((tt, GDN_WIDTH), lambda b, c: (b * nc + c, 0)),
        out_shape=jax.ShapeDtypeStruct((t, GDN_WIDTH), BF16),
        scratch_shapes=[
            pltpu.VMEM((tt + 8, 3 * GDN_WIDTH), F32),
            pltpu.VMEM((N_HEADS, HEAD_DIM, HEAD_DIM), F32),
        ],
        compiler_params=pltpu.CompilerParams(
            dimension_semantics=("parallel", "arbitrary"), vmem_limit_bytes=VMEM_LIMIT),
        name="gated_deltanet",
    )(proj, proj, small, small_t3, conv_w, expand, tri3, triup3, alog_row, dtb_row, alog_col, dtb_col, norm_g)


def _s5_kernel(xb_ref, zb_ref, perm_ref, permt_ref, bdb_ref, bdc_ref, tab_ref, d_ref, wglu_ref, o_ref,
               h_scr, carry_scr):
    tt = xb_ref.shape[0]
    nj = tt // 8
    ns = S5_GROUPS * S5_STATE // S5_SLABS

    @pl.when(pl.program_id(1) == 0)
    def _():
        carry_scr[...] = jnp.zeros_like(carry_scr)

    def cmul(ar, ai, br, bi):
        return ar * br - ai * bi, ar * bi + ai * br

    xp = _sel_lhs(perm_ref[...], xb_ref[...])
    xpb = xp.astype(BF16)
    not_first = lax.broadcasted_iota(jnp.int32, (8, ns), 0) >= 1

    ys = []
    for s in range(S5_SLABS):
        h_scr[s] = jnp.dot(xpb[:, s * 256:(s + 1) * 256], bdb_ref[s], preferred_element_type=F32)
        ar, ai = tab_ref[s, 0, 0], tab_ref[s, 0, 1]
        er = h_scr[s, 0:8, :ns]
        ei = h_scr[s, 0:8, ns:]
        for j in range(1, nj):
            mr, mi = cmul(ar, ai, er, ei)
            er = mr + h_scr[s, j * 8:(j + 1) * 8, :ns]
            ei = mi + h_scr[s, j * 8:(j + 1) * 8, ns:]
        for lvl, dist in enumerate((1, 2, 4)):
            mr, mi = cmul(tab_ref[s, 1 + lvl, 0], tab_ref[s, 1 + lvl, 1],
                          pltpu.roll(er, dist, 0), pltpu.roll(ei, dist, 0))
            er, ei = er + mr, ei + mi
        cr, ci = carry_scr[s, 0], carry_scr[s, 1]
        mr, mi = cmul(tab_ref[s, 4, 0], tab_ref[s, 4, 1], cr, ci)
        gr = jnp.where(not_first, pltpu.roll(er, 1, 0), 0.0) + mr
        gi = jnp.where(not_first, pltpu.roll(ei, 1, 0), 0.0) + mi
        mr, mi = cmul(tab_ref[s, 5, 0], tab_ref[s, 5, 1], cr, ci)
        carry_scr[s, 0] = jnp.broadcast_to(er[7:8, :], (8, ns)) + mr
        carry_scr[s, 1] = jnp.broadcast_to(ei[7:8, :], (8, ns)) + mi
        for j in range(nj):
            mr, mi = cmul(ar, ai, gr, gi)
            gr = mr + h_scr[s, j * 8:(j + 1) * 8, :ns]
            gi = mi + h_scr[s, j * 8:(j + 1) * 8, ns:]
            h_scr[s, j * 8:(j + 1) * 8, :ns] = gr
            h_scr[s, j * 8:(j + 1) * 8, ns:] = gi
        ys.append(jnp.dot(h_scr[s, :, :ns].astype(BF16), bdc_ref[s, :ns, :], preferred_element_type=F32)
                  + jnp.dot(h_scr[s, :, ns:].astype(BF16), bdc_ref[s, ns:, :], preferred_element_type=F32))

    y = _gelu_tanh(jnp.concatenate(ys, axis=1) + d_ref[...] * xp)
    glu = jnp.dot(y.astype(BF16), wglu_ref[...], preferred_element_type=F32)
    o_b = (glu[:, :S5_WIDTH] * _sigmoid(glu[:, S5_WIDTH:])).astype(BF16)
    o_b = jnp.dot(permt_ref[...], o_b, preferred_element_type=F32)
    o_ref[...] = (o_b * _silu(zb_ref[...])).astype(o_ref.dtype)


def _s5_branch(proj, bdb, bdc, tabs, d_row, w_glu, bsz, seq, tt=256):
    nt = seq // tt
    ns = S5_GROUPS * S5_STATE // S5_SLABS
    rho = jnp.arange(tt)
    perm = ((rho % 8) * (tt // 8) + rho // 8)[:, None] == jnp.arange(tt)[None, :]
    full = lambda shape: pl.BlockSpec(shape, lambda b, i: (0,) * len(shape))
    return pl.pallas_call(
        _s5_kernel,
        grid=(bsz, nt),
        in_specs=[
            pl.BlockSpec((tt, S5_WIDTH), lambda b, i: (b * nt + i, COL_XB)),
            pl.BlockSpec((tt, S5_WIDTH), lambda b, i: (b * nt + i, COL_ZB)),
            full((tt, 3 * tt)),
            full((tt, tt)),
            full((S5_SLABS, 256, 2 * ns)),
            full((S5_SLABS, 2 * ns, 256)),
            full((S5_SLABS, 6, 2, 8, ns)),
            full((1, S5_WIDTH)),
            full((S5_WIDTH, 2 * S5_WIDTH)),
        ],
        out_specs=pl.BlockSpec((tt, S5_WIDTH), lambda b, i: (b * nt + i, 0)),
        out_shape=jax.ShapeDtypeStruct((bsz * seq, S5_WIDTH), BF16),
        scratch_shapes=[
            pltpu.VMEM((S5_SLABS, tt, 2 * ns), F32),
            pltpu.VMEM((S5_SLABS, 2, 8, ns), F32),
        ],
        compiler_params=pltpu.CompilerParams(
            dimension_semantics=("parallel", "arbitrary"), vmem_limit_bytes=VMEM_LIMIT),
        name="s5_layer",
    )(proj, proj, jnp.tile(perm, (1, 3)).astype(BF16), perm.T.astype(BF16), bdb, bdc, tabs, d_row, w_glu)


def _s5_tables(lam_re, lam_im, log_dt, b_re, b_im, c_re, c_im, seg):
    g, p, gi = S5_GROUPS, S5_STATE, S5_GROUP
    gs = g // S5_SLABS
    lr, li = lam_re.astype(F32), lam_im.astype(F32)
    dt = jnp.exp(log_dt.astype(F32))[:, None]
    mag = jnp.exp(lr * dt)
    ab_re, ab_im = mag * jnp.cos(li * dt), mag * jnp.sin(li * dt)
    den = lr * lr + li * li
    nr, ni = ab_re - 1.0, ab_im
    coef_re = (nr * lr + ni * li) / den
    coef_im = (ni * lr - nr * li) / den
    br, bi = b_re.astype(F32), b_im.astype(F32)
    bb_re = coef_re[..., None] * br - coef_im[..., None] * bi
    bb_im = coef_re[..., None] * bi + coef_im[..., None] * br

    eye = jnp.eye(gs, dtype=F32)

    def blockdiag_in(bb):
        bb = bb.reshape(S5_SLABS, gs, p, gi)
        return jnp.einsum('sgpi,gh->sgihp', bb, eye).reshape(S5_SLABS, gs * gi, gs * p)

    def blockdiag_out(cc):
        cc = cc.reshape(S5_SLABS, gs, gi, p)
        return jnp.einsum('sgip,gh->sgphi', cc, eye).reshape(S5_SLABS, gs * p, gs * gi)

    bdb = jnp.concatenate([blockdiag_in(bb_re), blockdiag_in(bb_im)], axis=2).astype(BF16)
    bdc = jnp.concatenate([blockdiag_out(c_re.astype(F32)), -blockdiag_out(c_im.astype(F32))],
                          axis=1).astype(BF16)

    def cmul(a, b):
        return a[0] * b[0] - a[1] * b[1], a[0] * b[1] + a[1] * b[0]

    p1 = (ab_re.reshape(S5_SLABS, gs * p), ab_im.reshape(S5_SLABS, gs * p))
    pseg = p1
    for _ in range(int(math.log2(seg))):
        pseg = cmul(pseg, pseg)
    pseg2 = cmul(pseg, pseg)
    pseg4 = cmul(pseg2, pseg2)
    ptile = cmul(pseg4, pseg4)
    rows = jnp.arange(8)[None, None, :, None]

    def bcast(pw):
        return jnp.broadcast_to(jnp.stack([pw[0], pw[1]], axis=1)[:, :, None, :], (S5_SLABS, 2, 8, gs * p))

    def masked(pw, dist):
        return jnp.where(rows >= dist, bcast(pw), 0.0)

    pows = [(jnp.ones_like(p1[0]), jnp.zeros_like(p1[0]))]
    for _ in range(7):
        pows.append(cmul(pows[-1], pseg))
    ramp = jnp.stack([jnp.stack([pw[0] for pw in pows], axis=1),
                      jnp.stack([pw[1] for pw in pows], axis=1)], axis=1)
    tabs = jnp.stack([bcast(p1), masked(pseg, 1), masked(pseg2, 2), masked(pseg4, 4), ramp, bcast(ptile)],
                     axis=1)
    return bdb, bdc, tabs


def _kv_kernel(mem_ref, g_ref, w_ref, o_ref):
    m = mem_ref[0]
    ms = jnp.mean(m * m, axis=-1, keepdims=True)
    mn = ((m * lax.rsqrt(ms + EPS)) * g_ref[...]).astype(BF16)
    o_ref[0] = jnp.dot(mn, w_ref[...], preferred_element_type=F32).astype(o_ref.dtype)


def _memory_kv(mem, mem_norm_g, w_kv):
    bsz = mem.shape[0]
    return pl.pallas_call(
        _kv_kernel,
        grid=(bsz,),
        in_specs=[
            pl.BlockSpec((1, MEM_LEN, D_MODEL), lambda b: (b, 0, 0)),
            pl.BlockSpec((1, D_MODEL), lambda b: (0, 0)),
            pl.BlockSpec((D_MODEL, 2 * XA_WIDTH), lambda b: (0, 0)),
        ],
        out_specs=pl.BlockSpec((1, MEM_LEN, 2 * XA_WIDTH), lambda b: (b, 0, 0)),
        out_shape=jax.ShapeDtypeStruct((bsz, MEM_LEN, 2 * XA_WIDTH), BF16),
        compiler_params=pltpu.CompilerParams(
            dimension_semantics=("parallel",), vmem_limit_bytes=VMEM_LIMIT),
        name="memory_kv",
    )(mem, mem_norm_g, w_kv)


def _xattn_kernel(q_ref, zc_ref, kv_ref, o_ref):
    scale = XA_HEAD_DIM ** -0.5
    for h in range(XA_HEADS):
        lo = h * XA_HEAD_DIM
        q = q_ref[:, lo:lo + XA_HEAD_DIM].astype(BF16)
        k = kv_ref[0, :, lo:lo + XA_HEAD_DIM]
        v = kv_ref[0, :, XA_WIDTH + lo:XA_WIDTH + lo + XA_HEAD_DIM]
        s = lax.dot_general(q, k, (((1,), (1,)), ((), ())), preferred_element_type=F32) * scale
        e = jnp.exp(s - jnp.max(s, axis=-1, keepdims=True))
        p = e / jnp.sum(e, axis=-1, keepdims=True)
        o = jnp.dot(p.astype(BF16), v, preferred_element_type=F32)
        o_ref[:, lo:lo + XA_HEAD_DIM] = (o * _silu(zc_ref[:, lo:lo + XA_HEAD_DIM])).astype(o_ref.dtype)


def _xattn_branch(proj, kv, bsz, seq, tt=512):
    nt = seq // tt
    return pl.pallas_call(
        _xattn_kernel,
        grid=(bsz, nt),
        in_specs=[
            pl.BlockSpec((tt, XA_WIDTH), lambda b, i: (b * nt + i, COL_QC)),
            pl.BlockSpec((tt, XA_WIDTH), lambda b, i: (b * nt + i, COL_ZC)),
            pl.BlockSpec((1, MEM_LEN, 2 * XA_WIDTH), lambda b, i: (b, 0, 0)),
        ],
        out_specs=pl.BlockSpec((tt, XA_WIDTH), lambda b, i: (b * nt + i, 0)),
        out_shape=jax.ShapeDtypeStruct((bsz * seq, XA_WIDTH), BF16),
        compiler_params=pltpu.CompilerParams(
            dimension_semantics=("parallel", "parallel"), vmem_limit_bytes=VMEM_LIMIT),
        name="memory_xattn",
    )(proj, proj, kv)


def _merge_kernel(a_ref, b_ref, c_ref, g0_ref, g1_ref, g2_ref, x_ref, wa_ref, wb_ref, wc_ref,
                  wo_ref, fg_ref, o_ref):
    merged = _sigmoid(g0_ref[...]) * jnp.dot(a_ref[...], wa_ref[...], preferred_element_type=F32)
    merged = merged + _sigmoid(g1_ref[...]) * jnp.dot(b_ref[...], wb_ref[...], preferred_element_type=F32)
    merged = merged + _sigmoid(g2_ref[...]) * jnp.dot(c_ref[...], wc_ref[...], preferred_element_type=F32)
    h = x_ref[...] + jnp.dot(merged.astype(BF16), wo_ref[...], preferred_element_type=F32)
    ms = jnp.mean(h * h, axis=-1, keepdims=True)
    o_ref[...] = (h * lax.rsqrt(ms + EPS)) * fg_ref[...]


def _merge_output(a, b, c, proj, x2d, wa, wb, wc, wo, final_g, tm=256):
    t = x2d.shape[0]
    gb = COL_GATES // 2
    const = lambda shape: pl.BlockSpec(shape, lambda i: (0,) * len(shape), pipeline_mode=pl.Buffered(1))
    return pl.pallas_call(
        _merge_kernel,
        grid=(t // tm,),
        in_specs=[
            pl.BlockSpec((tm, GDN_WIDTH), lambda i: (i, 0)),
            pl.BlockSpec((tm, S5_WIDTH), lambda i: (i, 0)),
            pl.BlockSpec((tm, XA_WIDTH), lambda i: (i, 0)),
            pl.BlockSpec((tm, D_MODEL), lambda i: (i, gb)),
            pl.BlockSpec((tm, D_MODEL), lambda i: (i, gb + 1)),
            pl.BlockSpec((tm, D_MODEL), lambda i: (i, gb + 2)),
            pl.BlockSpec((tm, D_MODEL), lambda i: (i, 0)),
            const((GDN_WIDTH, D_MODEL)),
            const((S5_WIDTH, D_MODEL)),
            const((XA_WIDTH, D_MODEL)),
            const((D_MODEL, D_MODEL)),
            const((1, D_MODEL)),
        ],
        out_specs=pl.BlockSpec((tm, D_MODEL), lambda i: (i, 0)),
        out_shape=jax.ShapeDtypeStruct((t, D_MODEL), F32),
        compiler_params=pltpu.CompilerParams(
            dimension_semantics=("parallel",), vmem_limit_bytes=VMEM_LIMIT),
        name="merge_output",
    )(a, b, c, proj, proj, proj, x2d, wa, wb, wc, wo, final_g)


def kernel(x, mem, norm_g, w_in, conv_w, gdn_a_log, gdn_dt_bias, gdn_norm_g, s5_lambda_re, s5_lambda_im,
           s5_log_dt, s5_b_re, s5_b_im, s5_c_re, s5_c_im, s5_d, s5_w_glu, mem_norm_g, w_kv_mem, w_br_a,
           w_br_b, w_br_c, w_out, final_g):
    bsz, seq, d = x.shape
    t = bsz * seq
    h = x.reshape(t, d)
    assert norm_g.shape[0] == 1, "the merge kernel fuses the final RMSNorm: single layer only"
    for l in range(1):
        w_t = jnp.swapaxes(w_in, 1, 2).reshape(w_in.shape[2], d)
        n_a = 4 * GDN_WIDTH
        proj, small, small_t = _in_projection(h, norm_g[l][None, :], w_t, n_a, 2 * N_HEADS)

        small_t3 = small_t.reshape(2 * N_HEADS, t // CHUNK, CHUNK).transpose(1, 0, 2)
        o_a = _gdn_branch(proj, small, small_t3, conv_w[l], gdn_a_log[l], gdn_dt_bias[l], gdn_norm_g[l],
                          bsz, seq)

        bdb, bdc, tabs = _s5_tables(s5_lambda_re[l], s5_lambda_im[l], s5_log_dt[l], s5_b_re[l], s5_b_im[l],
                                    s5_c_re[l], s5_c_im[l], S5_TILE // 8)
        o_b = _s5_branch(proj, bdb, bdc, tabs, s5_d[l][None, :], s5_w_glu[l].astype(BF16), bsz, seq, S5_TILE)

        kv = _memory_kv(mem, mem_norm_g[l][None, :], w_kv_mem[l].astype(BF16))
        o_c = _xattn_branch(proj, kv, bsz, seq)

        out = _merge_output(o_a, o_b, o_c, proj, h, w_br_a[l].astype(BF16), w_br_b[l].astype(BF16),
                            w_br_c[l].astype(BF16), w_out[l].astype(BF16), final_g[None, :])
    return out.reshape(bsz, seq, d)
```
